```python
import jax, jax.numpy as jnp
from jax import lax
import numpy as np

D_MODEL = 1024
BATCH = 4
SEQ = 4096
DEPTH = 2
DEC_BATCH = 32
DEC_SEQ = 4
PAST_LEN = 8192
PAGE_SIZE = 128

N_HEADS = 8
HEAD_DIM = D_MODEL // 16
ATT_WIDTH = N_HEADS * HEAD_DIM
CONV_CH = D_MODEL // 2
CONV_WIDTH = 31
D_FF = ((8 * D_MODEL // 3 + 255) // 256) * 256
Q_BLOCK = 128
EPS = 1e-6
ATT_BIAS_INIT = -7.0
IN_WIDTH = 2 * CONV_CH + 3 * ATT_WIDTH + 2 * D_MODEL

kernel_name = "conformer_conv_stickbreaking_gated_hybrid_step"


def rms_norm(x, g):
    xf = x.astype(jnp.float32)
    y = xf * lax.rsqrt(jnp.mean(xf * xf, axis=-1, keepdims=True) + EPS)
    return (y * g.astype(jnp.float32)).astype(x.dtype)


def layer_norm(x, g, b):
    xf = x.astype(jnp.float32)
    mu = jnp.mean(xf, axis=-1, keepdims=True)
    var = jnp.mean(jnp.square(xf - mu), axis=-1, keepdims=True)
    y = (xf - mu) * lax.rsqrt(var + EPS)
    return (y * g.astype(jnp.float32) + b.astype(jnp.float32)).astype(x.dtype)


def stick_breaking_block(q, k, v, bias, q_pos, k_pos):
    z = jnp.einsum('bqhd,bshd->bhqs', q.astype(jnp.float32), k.astype(jnp.float32)) * (HEAD_DIM ** -0.5)
    z = z + bias.astype(jnp.float32)[None, :, None, None]
    mask = k_pos[None, :] < q_pos[:, None]
    c = jnp.where(mask, jax.nn.softplus(z), 0.0)
    rev = lax.cumsum(c, axis=3, reverse=True)
    log_a = jax.nn.log_sigmoid(z) - (rev - c)
    a = jnp.where(mask, jnp.exp(log_a), 0.0)
    o = jnp.einsum('bhqs,bshd->bqhd', a, v.astype(jnp.float32))
    return o.astype(v.dtype)


def stick_breaking(q, k, v, bias, q_pos, k_pos):
    b, t = q.shape[0], q.shape[1]
    if t > Q_BLOCK and t % Q_BLOCK == 0:
        nb = t // Q_BLOCK
        qb = q.reshape(b, nb, Q_BLOCK, N_HEADS, HEAD_DIM).transpose(1, 0, 2, 3, 4)
        pb = q_pos.reshape(nb, Q_BLOCK)
        ob = lax.map(lambda a: stick_breaking_block(a[0], k, v, bias, a[1], k_pos), (qb, pb))
        return ob.transpose(1, 0, 2, 3, 4).reshape(b, t, N_HEADS, HEAD_DIM)
    return stick_breaking_block(q, k, v, bias, q_pos, k_pos)


def conv_branch(u, buf, w_dw, b_dw, ln_g, ln_b, w_out):
    full = jnp.concatenate([buf.astype(u.dtype), u], axis=1)
    y = lax.conv_general_dilated(full, w_dw[:, None, :], window_strides=(1,), padding='VALID',
                                 dimension_numbers=('NWC', 'WIO', 'NWC'),
                                 feature_group_count=CONV_CH) + b_dw
    y = jax.nn.silu(layer_norm(y, ln_g, ln_b))
    return y @ w_out, full[:, -(CONV_WIDTH - 1):]


def layer(x, conv_buf, past_k, past_v, q_pos, k_pos,
          g_pre_mix, g_post_mix, g_pre_ffn, g_post_ffn, w_in, conv_dw, conv_b,
          conv_ln_g, conv_ln_b, w_conv_out, att_bias, w_att_out, w_o, w_ffn_in, w_ffn_out):
    b, t, _ = x.shape
    h = rms_norm(x, g_pre_mix)
    proj = h @ w_in
    o1 = 2 * CONV_CH
    o2 = o1 + 3 * ATT_WIDTH
    glu = proj[..., :o1]
    qkv = proj[..., o1:o2]
    gates = proj[..., o2:]
    u = glu[..., :CONV_CH] * jax.nn.sigmoid(glu[..., CONV_CH:])
    conv_d, new_buf = conv_branch(u, conv_buf, conv_dw, conv_b, conv_ln_g, conv_ln_b, w_conv_out)
    q, k, v = jnp.split(qkv, 3, axis=-1)
    q = q.reshape(b, t, N_HEADS, HEAD_DIM)
    k = k.reshape(b, t, N_HEADS, HEAD_DIM)
    v = v.reshape(b, t, N_HEADS, HEAD_DIM)
    if past_k is None:
        k_all, v_all = k, v
    else:
        k_all = jnp.concatenate([past_k.astype(k.dtype), k], axis=1)
        v_all = jnp.concatenate([past_v.astype(v.dtype), v], axis=1)
    att = stick_breaking(q, k_all, v_all, att_bias, q_pos, k_pos).reshape(b, t, ATT_WIDTH)
    att_d = att @ w_att_out
    merged = jax.nn.sigmoid(gates[..., :D_MODEL]) * conv_d + jax.nn.sigmoid(gates[..., D_MODEL:]) * att_d
    x = x + rms_norm(merged @ w_o, g_post_mix)
    hf = rms_norm(x, g_pre_ffn) @ w_ffn_in
    ff = (jax.nn.silu(hf[..., :D_FF]) * hf[..., D_FF:]) @ w_ffn_out
    x = x + rms_norm(ff, g_post_ffn)
    return x, k, v, new_buf


def setup_inputs(seed: int = 0) -> dict:
    key = jax.random.key(seed)
    ks = jax.random.split(key, 24)
    n_pages = PAST_LEN // PAGE_SIZE
    n_used = DEC_BATCH * n_pages
    n_phys = n_used + (n_used + 3) // 4
    nrm = lambda k, shape, s: jax.random.normal(k, shape, jnp.float32) * s
    page_table = jax.random.permutation(ks[5], n_phys)[:n_used].reshape(DEC_BATCH, n_pages).astype(jnp.int32)
    return {
        "x_prompt": nrm(ks[0], (BATCH, SEQ, D_MODEL), 1.0),
        "x_sample": nrm(ks[1], (DEC_BATCH, DEC_SEQ, D_MODEL), 1.0),
        "cache_k": nrm(ks[2], (DEPTH, n_phys, PAGE_SIZE, N_HEADS, HEAD_DIM), 1.0),
        "cache_v": nrm(ks[3], (DEPTH, n_phys, PAGE_SIZE, N_HEADS, HEAD_DIM), 1.0),
        "state_conv": nrm(ks[4], (DEPTH, DEC_BATCH, CONV_WIDTH - 1, CONV_CH), 0.5),
        "page_table": page_table,
        "norm_pre_mix": 1.0 + nrm(ks[6], (DEPTH, D_MODEL), 0.05),
        "norm_post_mix": 1.0 + nrm(ks[7], (DEPTH, D_MODEL), 0.05),
        "norm_pre_ffn": 1.0 + nrm(ks[8], (DEPTH, D_MODEL), 0.05),
        "norm_post_ffn": 1.0 + nrm(ks[9], (DEPTH, D_MODEL), 0.05),
        "w_in": nrm(ks[10], (DEPTH, D_MODEL, IN_WIDTH), D_MODEL ** -0.5),
        "conv_dw": nrm(ks[11], (DEPTH, CONV_WIDTH, CONV_CH), CONV_WIDTH ** -0.5),
        "conv_b": nrm(ks[12], (DEPTH, CONV_CH), 0.01),
        "conv_ln_g": 1.0 + nrm(ks[13], (DEPTH, CONV_CH), 0.05),
        "conv_ln_b": nrm(ks[14], (DEPTH, CONV_CH), 0.01),
        "w_conv_out": nrm(ks[15], (DEPTH, CONV_CH, D_MODEL), CONV_CH ** -0.5),
        "att_bias": ATT_BIAS_INIT + nrm(ks[20], (DEPTH, N_HEADS), 0.1),
        "w_att_out": nrm(ks[16], (DEPTH, ATT_WIDTH, D_MODEL), ATT_WIDTH ** -0.5),
        "w_o": nrm(ks[17], (DEPTH, D_MODEL, D_MODEL), D_MODEL ** -0.5),
        "w_ffn_in": nrm(ks[18], (DEPTH, D_MODEL, 2 * D_FF), D_MODEL ** -0.5),
        "w_ffn_out": nrm(ks[19], (DEPTH, D_FF, D_MODEL), D_FF ** -0.5),
    }


def reference(x_prompt, x_sample, cache_k, cache_v, state_conv, page_table,
              norm_pre_mix, norm_post_mix, norm_pre_ffn, norm_post_ffn, w_in, conv_dw, conv_b,
              conv_ln_g, conv_ln_b, w_conv_out, att_bias, w_att_out, w_o, w_ffn_in, w_ffn_out):
    t_p = x_prompt.shape[1]
    t_s = x_sample.shape[1]
    dec_b = x_sample.shape[0]
    n_pages = PAST_LEN // PAGE_SIZE
    pos_p = jnp.arange(t_p, dtype=jnp.int32)
    q_pos_s = PAST_LEN + jnp.arange(t_s, dtype=jnp.int32)
    k_pos_s = jnp.arange(PAST_LEN + t_s, dtype=jnp.int32)
    zero_buf = jnp.zeros((x_prompt.shape[0], CONV_WIDTH - 1, CONV_CH), x_prompt.dtype)

    hp, hs = x_prompt, x_sample
    kp_l, vp_l, cp_l, ks_l, vs_l, cs_l = [], [], [], [], [], []
    for l in range(DEPTH):
        params = (norm_pre_mix[l], norm_post_mix[l], norm_pre_ffn[l], norm_post_ffn[l], w_in[l],
                  conv_dw[l], conv_b[l], conv_ln_g[l], conv_ln_b[l], w_conv_out[l], att_bias[l],
                  w_att_out[l], w_o[l], w_ffn_in[l], w_ffn_out[l])
        hp, kp, vp, cp = layer(hp, zero_buf, None, None, pos_p, pos_p, *params)
        past_k = cache_k[l][page_table].reshape(dec_b, n_pages * PAGE_SIZE, N_HEADS, HEAD_DIM)
        past_v = cache_v[l][page_table].reshape(dec_b, n_pages * PAGE_SIZE, N_HEADS, HEAD_DIM)
        hs, kn, vn, cn = layer(hs, state_conv[l], past_k, past_v, q_pos_s, k_pos_s, *params)
        kp_l.append(kp); vp_l.append(vp); cp_l.append(cp)
        ks_l.append(kn); vs_l.append(vn); cs_l.append(cn)

    k_prompt = jnp.stack(kp_l)
    v_prompt = jnp.stack(vp_l)
    conv_prompt = jnp.stack(cp_l)
    k_sample = jnp.stack(ks_l)
    v_sample = jnp.stack(vs_l)
    conv_sample = jnp.stack(cs_l)
    return (hp, hs, k_prompt, v_prompt, conv_prompt, k_sample, v_sample, conv_sample)
```

```python
import functools

import jax
import jax.numpy as jnp
from jax import lax
from jax.experimental import pallas as pl
from jax.experimental.pallas import tpu as pltpu

D_MODEL = 1024
N_HEADS = 8
HEAD_DIM = 64
ATT_WIDTH = N_HEADS * HEAD_DIM
CONV_CH = 512
CONV_WIDTH = 31
CONV_HIST = CONV_WIDTH - 1
D_FF = 2816
EPS = 1e-6
PAGE_SIZE = 128
PAST_LEN = 8192
Q_SCALE = HEAD_DIM ** -0.5

LANES = 128
HEADS_PER_LANE_TILE = LANES // HEAD_DIM
N_HEAD_PAIRS = ATT_WIDTH // LANES
ATT_BLOCK = 256
CONV_ROWS = 64
CONV_PAD = 32
PAGES_PER_STEP = 8
VMEM_LIMIT = 56 * 1024 * 1024

F32 = jnp.float32
BF16 = jnp.bfloat16


def _rms(x, g):
    ms = jnp.mean(x * x, axis=-1, keepdims=True)
    return (x * lax.rsqrt(ms + EPS)) * g


def _softplus(z):
    return jnp.maximum(z, 0.0) + jnp.log(1.0 + jnp.exp(-jnp.abs(z)))


def _const_spec(shape):
    return pl.BlockSpec(shape, lambda *_: (0,) * len(shape), pipeline_mode=pl.Buffered(1))


def _inproj_body(x_ref, g_ref, wglu_ref, wqkv_ref, u_ref, q_ref, k_ref, v_ref, kb_ref, vb_ref):
    h = _rms(x_ref[...], g_ref[...]).astype(BF16)
    glu = jnp.dot(h, wglu_ref[...], preferred_element_type=F32)
    u_ref[...] = glu[:, :CONV_CH] * jax.nn.sigmoid(glu[:, CONV_CH:])
    qkv = jnp.dot(h, wqkv_ref[...], preferred_element_type=F32)
    q_ref[...] = (qkv[:, :ATT_WIDTH] * Q_SCALE).astype(BF16)
    k = qkv[:, ATT_WIDTH:2 * ATT_WIDTH]
    v = qkv[:, 2 * ATT_WIDTH:]
    k_ref[...] = k
    v_ref[...] = v
    kb_ref[...] = k.astype(BF16)
    vb_ref[...] = v.astype(BF16)


def _inproj(x, g, wglu, wqkv, tm):
    t = x.shape[0]
    tok = lambda w, dt: jax.ShapeDtypeStruct((t, w), dt)
    tspec = lambda w: pl.BlockSpec((tm, w), lambda i: (i, 0))
    return pl.pallas_call(
        _inproj_body,
        grid=(t // tm,),
        in_specs=[tspec(D_MODEL), _const_spec((1, D_MODEL)),
                  _const_spec(wglu.shape), _const_spec(wqkv.shape)],
        out_specs=[tspec(CONV_CH), tspec(ATT_WIDTH), tspec(ATT_WIDTH), tspec(ATT_WIDTH),
                   tspec(ATT_WIDTH), tspec(ATT_WIDTH)],
        out_shape=[tok(CONV_CH, F32), tok(ATT_WIDTH, BF16), tok(ATT_WIDTH, F32), tok(ATT_WIDTH, F32),
                   tok(ATT_WIDTH, BF16), tok(ATT_WIDTH, BF16)],
        compiler_params=pltpu.CompilerParams(dimension_semantics=("arbitrary",),
                                             vmem_limit_bytes=VMEM_LIMIT),
        name="inproj",
    )(x, g, wglu, wqkv)


def _ln_swish(y, lg, lb):
    mu = jnp.mean(y, axis=-1, keepdims=True)
    d = y - mu
    var = jnp.mean(d * d, axis=-1, keepdims=True)
    yn = (d * lax.rsqrt(var + EPS)) * lg + lb
    return yn * jax.nn.sigmoid(yn)


def _conv_prompt_body(u_ref, w_ref, b_ref, lg_ref, lb_ref, y_ref, hist_ref, buf_ref, *, tc):
    t = pl.program_id(1)

    @pl.when(t == 0)
    def _():
        buf_ref[0:CONV_PAD, :] = jnp.zeros((CONV_PAD, CONV_CH), F32)

    @pl.when(t > 0)
    def _():
        buf_ref[0:CONV_PAD, :] = buf_ref[tc:tc + CONV_PAD, :]

    buf_ref[CONV_PAD:CONV_PAD + tc, :] = u_ref[...]
    first_tap = CONV_PAD - CONV_HIST

    for base in range(0, tc, CONV_ROWS):
        acc = jnp.broadcast_to(b_ref[...], (CONV_ROWS, CONV_CH))
        for j in range(CONV_WIDTH):
            lo = base + first_tap + j
            acc = acc + w_ref[j:j + 1, :] * buf_ref[lo:lo + CONV_ROWS, :]
        y_ref[base:base + CONV_ROWS, :] = _ln_swish(acc, lg_ref[...], lb_ref[...]).astype(BF16)

    @pl.when(t == pl.num_programs(1) - 1)
    def _():
        hist_ref[...] = buf_ref[CONV_PAD + tc - CONV_HIST:CONV_PAD + tc, :]


def _conv_prompt(u, w, b, lg, lb, tc):
    bsz, t, _ = u.shape
    return pl.pallas_call(
        functools.partial(_conv_prompt_body, tc=tc),
        grid=(bsz, t // tc),
        in_specs=[pl.BlockSpec((None, tc, CONV_CH), lambda bi, ti: (bi, ti, 0)),
                  _const_spec((CONV_WIDTH, CONV_CH)), _const_spec((1, CONV_CH)),
                  _const_spec((1, CONV_CH)), _const_spec((1, CONV_CH))],
        out_specs=[pl.BlockSpec((None, tc, CONV_CH), lambda bi, ti: (bi, ti, 0)),
                   pl.BlockSpec((None, CONV_HIST, CONV_CH), lambda bi, ti: (bi, 0, 0))],
        out_shape=[jax.ShapeDtypeStruct((bsz, t, CONV_CH), BF16),
                   jax.ShapeDtypeStruct((bsz, CONV_HIST, CONV_CH), F32)],
        scratch_shapes=[pltpu.VMEM((CONV_PAD + tc, CONV_CH), F32)],
        compiler_params=pltpu.CompilerParams(dimension_semantics=("arbitrary", "arbitrary"),
                                             vmem_limit_bytes=VMEM_LIMIT),
        name="conv_prompt",
    )(u, w, b, lg, lb)


def _conv_sample_body(u_ref, st_ref, w_ref, b_ref, lg_ref, lb_ref, y_ref, hist_ref, buf_ref, *, ts):
    buf_ref[:, 0:CONV_HIST, :] = st_ref[...]
    buf_ref[:, CONV_HIST:CONV_HIST + ts, :] = u_ref[...]
    nseq = u_ref.shape[0]
    acc = jnp.broadcast_to(b_ref[...][None], (nseq, ts, CONV_CH))
    for j in range(CONV_WIDTH):
        acc = acc + w_ref[j:j + 1, :][None] * buf_ref[:, j:j + ts, :]
    y_ref[...] = _ln_swish(acc, lg_ref[...][None], lb_ref[...][None])
    hist_ref[...] = buf_ref[:, ts:ts + CONV_HIST, :]


def _conv_sample(u, state, w, b, lg, lb):
    nseq, ts, _ = u.shape
    full = lambda s: pl.BlockSpec(s, lambda i: (0,) * len(s))
    return pl.pallas_call(
        functools.partial(_conv_sample_body, ts=ts),
        grid=(1,),
        in_specs=[full(u.shape), full(state.shape), full(w.shape), full(b.shape), full(lg.shape),
                  full(lb.shape)],
        out_specs=[full(u.shape), full(state.shape)],
        out_shape=[jax.ShapeDtypeStruct(u.shape, F32), jax.ShapeDtypeStruct(state.shape, F32)],
        scratch_shapes=[pltpu.VMEM((nseq, CONV_HIST + ts, CONV_CH), F32)],
        compiler_params=pltpu.CompilerParams(dimension_semantics=("arbitrary",),
                                             vmem_limit_bytes=VMEM_LIMIT),
        name="conv_sample",
    )(u, state, w, b, lg, lb)


def _attn_prompt_body(bias_ref, q_ref, k_ref, v_ref, tri_ref, o_ref):
    hp = pl.program_id(1)
    qi = pl.program_id(2)
    blk = ATT_BLOCK
    q2 = q_ref[...]
    lane = lax.broadcasted_iota(jnp.int32, (blk, LANES), 1)
    row = lax.broadcasted_iota(jnp.int32, (blk, blk), 0)
    col = lax.broadcasted_iota(jnp.int32, (blk, blk), 1)
    causal = col < row
    tri = tri_ref[...]
    outs = []
    for hh in range(HEADS_PER_LANE_TILE):
        in_head = (lane >= hh * HEAD_DIM) & (lane < (hh + 1) * HEAD_DIM)
        qm = jnp.where(in_head, q2, jnp.zeros_like(q2))
        bias = bias_ref[hp * HEADS_PER_LANE_TILE + hh]

        def block(j, run, o, masked, qm=qm, bias=bias):
            start = pl.multiple_of(j * blk, blk)
            kj = k_ref[pl.ds(start, blk), :]
            vj = v_ref[pl.ds(start, blk), :]
            z = lax.dot_general(qm, kj, (((1,), (1,)), ((), ())), preferred_element_type=F32) + bias
            c = _softplus(z)
            if masked:
                c = jnp.where(causal, c, 0.0)
            later = jnp.dot(c.astype(BF16), tri, preferred_element_type=F32)
            a = jnp.exp((z - c) - later - run)
            if masked:
                a = jnp.where(causal, a, 0.0)
            o = o + jnp.dot(a.astype(BF16), vj, preferred_element_type=F32)
            run = run + jnp.sum(c, axis=-1, keepdims=True)
            return run, o

        run, o = block(qi, jnp.zeros((blk, 1), F32), jnp.zeros((blk, LANES), F32), True)
        run, o = lax.fori_loop(0, qi, lambda i, rc: block(qi - 1 - i, rc[0], rc[1], False), (run, o))
        outs.append(o)
    o_ref[...] = jnp.where(lane < HEAD_DIM, outs[0], outs[1]).astype(BF16)


def _attn_prompt(bias, q, kb, vb, tri):
    bsz, t, _ = q.shape
    blk = ATT_BLOCK
    return pl.pallas_call(
        _attn_prompt_body,
        grid=(bsz, N_HEAD_PAIRS, t // blk),
        in_specs=[pl.BlockSpec(memory_space=pltpu.SMEM),
                  pl.BlockSpec((None, blk, LANES), lambda b, h, i: (b, i, h)),
                  pl.BlockSpec((None, t, LANES), lambda b, h, i: (b, 0, h)),
                  pl.BlockSpec((None, t, LANES), lambda b, h, i: (b, 0, h)),
                  _const_spec((blk, blk))],
        out_specs=pl.BlockSpec((None, blk, LANES), lambda b, h, i: (b, i, h)),
        out_shape=jax.ShapeDtypeStruct((bsz, t, ATT_WIDTH), BF16),
        compiler_params=pltpu.CompilerParams(
            dimension_semantics=("arbitrary", "arbitrary", "arbitrary"), vmem_limit_bytes=VMEM_LIMIT),
        name="attn_prompt",
    )(bias, q, kb, vb, tri)


def _attn_sample_body(pt_ref, qbd_ref, biasr_ref, knew_ref, vnew_ref, tri_ref, hm_ref, *rest, ts):
    del pt_ref
    k_refs = rest[:PAGES_PER_STEP]
    v_refs = rest[PAGES_PER_STEP:2 * PAGES_PER_STEP]
    o_ref, run_ref, acc_ref = rest[2 * PAGES_PER_STEP:]
    g = pl.program_id(1)
    qbd = qbd_ref[...]
    biasr = biasr_ref[...]
    tri = tri_ref[...]

    def page(kt, vt, mask):
        z = jnp.dot(qbd, kt, preferred_element_type=F32) + biasr
        c = _softplus(z)
        if mask is not None:
            c = jnp.where(mask, c, 0.0)
        sums = jnp.dot(c.astype(BF16), tri, preferred_element_type=F32)
        later = sums[:, :PAGE_SIZE]
        total = sums[:, PAGE_SIZE:]
        a = jnp.exp((z - c) - later - run_ref[...])
        if mask is not None:
            a = jnp.where(mask, a, 0.0)
        acc_ref[...] += lax.dot_general(a, vt, (((1,), (1,)), ((), ())), preferred_element_type=F32)
        run_ref[...] += total

    @pl.when(g == 0)
    def _():
        run_ref[...] = jnp.zeros_like(run_ref)
        acc_ref[...] = jnp.zeros_like(acc_ref)
        shape = (ts * N_HEADS, PAGE_SIZE)
        key = lax.broadcasted_iota(jnp.int32, shape, 1)
        qidx = lax.broadcasted_iota(jnp.int32, shape, 0) // N_HEADS
        page(knew_ref[...], vnew_ref[...], (key < qidx) & (key < ts))

    for p in reversed(range(PAGES_PER_STEP)):
        page(k_refs[p][...], v_refs[p][...], None)

    @pl.when(g == pl.num_programs(1) - 1)
    def _():
        res = acc_ref[...].reshape(ts, N_HEADS, ATT_WIDTH) * hm_ref[...][None]
        o_ref[...] = jnp.sum(res, axis=1)


def _attn_sample(page_table, qbd, biasr, knew, vnew, tri, hm, cache_kt, cache_vt, layer, ts):
    nseq, n_pages = page_table.shape
    n_steps = n_pages // PAGES_PER_STEP
    nrow = ts * N_HEADS

    def page_spec(p):
        def imap(b, g, pt):
            return (layer, pt[b, (n_steps - 1 - g) * PAGES_PER_STEP + p], 0, 0)
        return pl.BlockSpec((None, None, ATT_WIDTH, PAGE_SIZE), imap)

    per_seq = lambda s: pl.BlockSpec((None,) + s, lambda b, g, pt: (b,) + (0,) * len(s))
    const = lambda s: pl.BlockSpec(s, lambda b, g, pt: (0,) * len(s))
    grid_spec = pltpu.PrefetchScalarGridSpec(
        num_scalar_prefetch=1,
        grid=(nseq, n_steps),
        in_specs=[per_seq((nrow, ATT_WIDTH)), const((nrow, PAGE_SIZE)),
                  per_seq((ATT_WIDTH, PAGE_SIZE)), per_seq((ATT_WIDTH, PAGE_SIZE)),
                  const((PAGE_SIZE, 2 * PAGE_SIZE)), const((N_HEADS, ATT_WIDTH))]
                 + [page_spec(p) for p in range(PAGES_PER_STEP)] * 2,
        out_specs=per_seq((ts, ATT_WIDTH)),
        scratch_shapes=[pltpu.VMEM((nrow, PAGE_SIZE), F32), pltpu.VMEM((nrow, ATT_WIDTH), F32)],
    )
    return pl.pallas_call(
        functools.partial(_attn_sample_body, ts=ts),
        grid_spec=grid_spec,
        out_shape=jax.ShapeDtypeStruct((nseq, ts, ATT_WIDTH), F32),
        compiler_params=pltpu.CompilerParams(dimension_semantics=("arbitrary", "arbitrary"),
                                             vmem_limit_bytes=VMEM_LIMIT),
        name="attn_sample",
    )(page_table, qbd, biasr, knew, vnew, tri, hm,
      *([cache_kt] * PAGES_PER_STEP), *([cache_vt] * PAGES_PER_STEP))


def _mix_body(x_ref, y_ref, att_ref, gpre_ref, gpost_ref, wg_ref, wc_ref, wa_ref, wo_ref, o_ref):
    x = x_ref[...]
    h = _rms(x, gpre_ref[...]).astype(BF16)
    gates = jnp.dot(h, wg_ref[...], preferred_element_type=F32)
    conv_d = jnp.dot(y_ref[...].astype(BF16), wc_ref[...], preferred_element_type=F32)
    att_d = jnp.dot(att_ref[...].astype(BF16), wa_ref[...], preferred_element_type=F32)
    merged = (jax.nn.sigmoid(gates[:, :D_MODEL]) * conv_d
              + jax.nn.sigmoid(gates[:, D_MODEL:]) * att_d)
    mo = jnp.dot(merged.astype(BF16), wo_ref[...], preferred_element_type=F32)
    o_ref[...] = x + _rms(mo, gpost_ref[...])


def _mix(x, y, att, gpre, gpost, wg, wc, wa, wo, tm):
    t = x.shape[0]
    tspec = lambda w: pl.BlockSpec((tm, w), lambda i: (i, 0))
    return pl.pallas_call(
        _mix_body,
        grid=(t // tm,),
        in_specs=[tspec(D_MODEL), tspec(CONV_CH), tspec(ATT_WIDTH),
                  _const_spec((1, D_MODEL)), _const_spec((1, D_MODEL)),
                  _const_spec(wg.shape), _const_spec(wc.shape), _const_spec(wa.shape),
                  _const_spec(wo.shape)],
        out_specs=tspec(D_MODEL),
        out_shape=jax.ShapeDtypeStruct((t, D_MODEL), F32),
        compiler_params=pltpu.CompilerParams(dimension_semantics=("arbitrary",),
                                             vmem_limit_bytes=VMEM_LIMIT),
        name="mix",
    )(x, y, att, gpre, gpost, wg, wc, wa, wo)


def _ffn_body(x_ref, gpre_ref, gpost_ref, w1_ref, w2_ref, o_ref):
    x = x_ref[...]
    h = _rms(x, gpre_ref[...]).astype(BF16)
    hf = jnp.dot(h, w1_ref[...], preferred_element_type=F32)
    a = hf[:, :D_FF]
    ff = (a * jax.nn.sigmoid(a)) * hf[:, D_FF:]
    out = jnp.dot(ff.astype(BF16), w2_ref[...], preferred_element_type=F32)
    o_ref[...] = x + _rms(out, gpost_ref[...])


def _ffn(x, gpre, gpost, w1, w2, tm):
    t = x.shape[0]
    tspec = pl.BlockSpec((tm, D_MODEL), lambda i: (i, 0))
    return pl.pallas_call(
        _ffn_body,
        grid=(t // tm,),
        in_specs=[tspec, _const_spec((1, D_MODEL)), _const_spec((1, D_MODEL)),
                  _const_spec(w1.shape), _const_spec(w2.shape)],
        out_specs=tspec,
        out_shape=jax.ShapeDtypeStruct((t, D_MODEL), F32),
        compiler_params=pltpu.CompilerParams(dimension_semantics=("arbitrary",),
                                             vmem_limit_bytes=VMEM_LIMIT),
        name="ffn",
    )(x, gpre, gpost, w1, w2)


def _strict_upper(n):
    r = lax.broadcasted_iota(jnp.int32, (n, n), 0)
    c = lax.broadcasted_iota(jnp.int32, (n, n), 1)
    return (r > c).astype(BF16)


def kernel(x_prompt, x_sample, cache_k, cache_v, state_conv, page_table, norm_pre_mix, norm_post_mix,
           norm_pre_ffn, norm_post_ffn, w_in, conv_dw, conv_b, conv_ln_g, conv_ln_b, w_conv_out,
           att_bias, w_att_out, w_o, w_ffn_in, w_ffn_out):
    depth = w_in.shape[0]
    bp, tp, _ = x_prompt.shape
    bs, ts, _ = x_sample.shape
    n_phys = cache_k.shape[1]
    tm_p = 512
    tm_s = bs * ts

    hp = x_prompt.reshape(bp * tp, D_MODEL)
    hs = x_sample.reshape(bs * ts, D_MODEL)
    ckt = jnp.transpose(cache_k, (0, 1, 3, 4, 2)).reshape(depth, n_phys, ATT_WIDTH, PAGE_SIZE)
    cvt = jnp.transpose(cache_v, (0, 1, 3, 4, 2)).reshape(depth, n_phys, ATT_WIDTH, PAGE_SIZE)

    tri_p = _strict_upper(ATT_BLOCK)
    tri_s = jnp.concatenate([_strict_upper(PAGE_SIZE), jnp.ones((PAGE_SIZE, PAGE_SIZE), BF16)], axis=1)
    head_of_col = jnp.arange(ATT_WIDTH, dtype=jnp.int32) // HEAD_DIM
    hm = (head_of_col[None, :] == jnp.arange(N_HEADS, dtype=jnp.int32)[:, None]).astype(F32)

    row = lambda a: a.reshape(1, -1)
    kp_l, vp_l, cp_l, ks_l, vs_l, cs_l = [], [], [], [], [], []
    o1 = 2 * CONV_CH
    o2 = o1 + 3 * ATT_WIDTH
    for l in range(depth):
        wglu = w_in[l, :, :o1].astype(BF16)
        wqkv = w_in[l, :, o1:o2].astype(BF16)
        wg = w_in[l, :, o2:].astype(BF16)
        wc = w_conv_out[l].astype(BF16)
        wa = w_att_out[l].astype(BF16)
        wo = w_o[l].astype(BF16)
        w1 = w_ffn_in[l].astype(BF16)
        w2 = w_ffn_out[l].astype(BF16)
        gpre, gpost = row(norm_pre_mix[l]), row(norm_post_mix[l])
        gpre_f, gpost_f = row(norm_pre_ffn[l]), row(norm_post_ffn[l])
        cw, cb = conv_dw[l], row(conv_b[l])
        lg, lb = row(conv_ln_g[l]), row(conv_ln_b[l])
        bias = att_bias[l].astype(F32)

        u, q, k, v, kb, vb = _inproj(hp, gpre, wglu, wqkv, tm_p)
        y, hist = _conv_prompt(u.reshape(bp, tp, CONV_CH), cw, cb, lg, lb, 256)
        att = _attn_prompt(bias, q.reshape(bp, tp, ATT_WIDTH), kb.reshape(bp, tp, ATT_WIDTH),
                           vb.reshape(bp, tp, ATT_WIDTH), tri_p)
        hp = _mix(hp, y.reshape(bp * tp, CONV_CH), att.reshape(bp * tp, ATT_WIDTH),
                  gpre, gpost, wg, wc, wa, wo, tm_p)
        hp = _ffn(hp, gpre_f, gpost_f, w1, w2, 256)
        kp_l.append(k.reshape(bp, tp, N_HEADS, HEAD_DIM))
        vp_l.append(v.reshape(bp, tp, N_HEADS, HEAD_DIM))
        cp_l.append(hist)

        u, q, k, v, _, _ = _inproj(hs, gpre, wglu, wqkv, tm_s)
        y, hist = _conv_sample(u.reshape(bs, ts, CONV_CH), state_conv[l], cw, cb, lg, lb)
        qbd = (q.astype(F32).reshape(bs, ts, 1, ATT_WIDTH) * hm[None, None]).reshape(
            bs, ts * N_HEADS, ATT_WIDTH)
        biasr = jnp.broadcast_to(jnp.tile(bias, ts)[:, None], (ts * N_HEADS, PAGE_SIZE))
        pad_keys = ((0, 0), (0, 0), (0, PAGE_SIZE - ts))
        knew = jnp.pad(jnp.swapaxes(k.reshape(bs, ts, ATT_WIDTH), 1, 2), pad_keys)
        vnew = jnp.pad(jnp.swapaxes(v.reshape(bs, ts, ATT_WIDTH), 1, 2), pad_keys)
        att = _attn_sample(page_table, qbd, biasr, knew, vnew, tri_s, hm, ckt, cvt, l, ts)
        hs = _mix(hs, y.reshape(bs * ts, CONV_CH), att.reshape(bs * ts, ATT_WIDTH),
                  gpre, gpost, wg, wc, wa, wo, tm_s)
        hs = _ffn(hs, gpre_f, gpost_f, w1, w2, tm_s)
        ks_l.append(k.reshape(bs, ts, N_HEADS, HEAD_DIM))
        vs_l.append(v.reshape(bs, ts, N_HEADS, HEAD_DIM))
        cs_l.append(hist)

    return (hp.reshape(bp, tp, D_MODEL), hs.reshape(bs, ts, D_MODEL),
            jnp.stack(kp_l), jnp.stack(vp_l), jnp.stack(cp_l),
            jnp.stack(ks_l), jnp.stack(vs_l), jnp.stack(cs_l))
```

```python
import functools

import jax
import jax.numpy as jnp
from jax import lax
from jax.experimental import pallas as pl
from jax.experimental.pallas import tpu as pltpu

D_MODEL = 1024
N_HEADS = 8
HEAD_DIM = 64
ATT_WIDTH = N_HEADS * HEAD_DIM
CONV_CH = 512
CONV_WIDTH = 31
CONV_HIST = CONV_WIDTH - 1
D_FF = 2816
EPS = 1e-6
PAGE_SIZE = 128
PAST_LEN = 8192
LOG2E = 1.4426950408889634
LN2 = 0.6931471805599453
Q_SCALE = HEAD_DIM ** -0.5 * LOG2E

LANES = 128
SUBLANES = 8
HEADS_PER_LANE_TILE = LANES // HEAD_DIM
N_HEAD_PAIRS = ATT_WIDTH // LANES
ATT_BLOCK = 256
ATT_QSUB = 2
CONV_ROWS = 64
CONV_PAD = 32
PAGES_PER_STEP = 8
VMEM_LIMIT = 56 * 1024 * 1024

F32 = jnp.float32
BF16 = jnp.bfloat16


def _rms(x, g):
    ms = jnp.mean(x * x, axis=-1, keepdims=True)
    return (x * lax.rsqrt(ms + EPS)) * g


_exp2 = jnp.exp2


def _softplus2(z):
    return jnp.maximum(z, 0.0) + jnp.log2(1.0 + jnp.exp2(jnp.abs(z) * (-1.0)))


def _const_spec(shape):
    return pl.BlockSpec(shape, lambda *_: (0,) * len(shape), pipeline_mode=pl.Buffered(1))


def _inproj_body(x_ref, g_ref, wglu_ref, wqkv_ref, u_ref, q_ref, k_ref, v_ref, kb_ref, vb_ref):
    h = _rms(x_ref[...], g_ref[...]).astype(BF16)
    glu = jnp.dot(h, wglu_ref[...], preferred_element_type=F32)
    u_ref[...] = glu[:, :CONV_CH] * jax.nn.sigmoid(glu[:, CONV_CH:])
    qkv = jnp.dot(h, wqkv_ref[...], preferred_element_type=F32)
    q_ref[...] = (qkv[:, :ATT_WIDTH] * Q_SCALE).astype(BF16)
    k = qkv[:, ATT_WIDTH:2 * ATT_WIDTH]
    v = qkv[:, 2 * ATT_WIDTH:]
    k_ref[...] = k
    v_ref[...] = v
    kb_ref[...] = k.astype(BF16)
    vb_ref[...] = v.astype(BF16)


def _inproj(x, g, wglu, wqkv, tm):
    t = x.shape[0]
    tok = lambda w, dt: jax.ShapeDtypeStruct((t, w), dt)
    tspec = lambda w: pl.BlockSpec((tm, w), lambda i: (i, 0))
    return pl.pallas_call(
        _inproj_body,
        grid=(t // tm,),
        in_specs=[tspec(D_MODEL), _const_spec((1, D_MODEL)),
                  _const_spec(wglu.shape), _const_spec(wqkv.shape)],
        out_specs=[tspec(CONV_CH), tspec(ATT_WIDTH), tspec(ATT_WIDTH), tspec(ATT_WIDTH),
                   tspec(ATT_WIDTH), tspec(ATT_WIDTH)],
        out_shape=[tok(CONV_CH, F32), tok(ATT_WIDTH, BF16), tok(ATT_WIDTH, F32), tok(ATT_WIDTH, F32),
                   tok(ATT_WIDTH, BF16), tok(ATT_WIDTH, BF16)],
        compiler_params=pltpu.CompilerParams(dimension_semantics=("arbitrary",),
                                             vmem_limit_bytes=VMEM_LIMIT),
        name="inproj",
    )(x, g, wglu, wqkv)


def _ln_swish(y, lg, lb):
    mu = jnp.mean(y, axis=-1, keepdims=True)
    d = y - mu
    var = jnp.mean(d * d, axis=-1, keepdims=True)
    yn = (d * lax.rsqrt(var + EPS)) * lg + lb
    return yn * jax.nn.sigmoid(yn)


def _conv_prompt_body(u_ref, w_ref, b_ref, lg_ref, lb_ref, y_ref, hist_ref, buf_ref, *, tc):
    t = pl.program_id(1)

    @pl.when(t == 0)
    def _():
        buf_ref[0:CONV_PAD, :] = jnp.zeros((CONV_PAD, CONV_CH), F32)

    @pl.when(t > 0)
    def _():
        buf_ref[0:CONV_PAD, :] = buf_ref[tc:tc + CONV_PAD, :]

    buf_ref[CONV_PAD:CONV_PAD + tc, :] = u_ref[...]
    first_tap = CONV_PAD - CONV_HIST

    shifts = [first_tap + j for j in range(CONV_WIDTH)]
    by_rho = {rho: [s for s in shifts if s % SUBLANES == rho] for rho in range(SUBLANES)}
    win = max(shifts) + CONV_ROWS
    for base in range(0, tc, CONV_ROWS):
        accs = []
        for lt in range(CONV_CH // LANES):
            cols = slice(lt * LANES, (lt + 1) * LANES)
            window = buf_ref[base:base + win, cols]
            acc = jnp.broadcast_to(b_ref[:, cols], (CONV_ROWS, LANES))
            for rho, group in by_rho.items():
                if not group:
                    continue
                shifted = pltpu.roll(window, win - rho, 0) if rho else window
                for s in group:
                    j = s - first_tap
                    acc = acc + w_ref[j:j + 1, cols] * shifted[s - rho:s - rho + CONV_ROWS, :]
            accs.append(acc)
        y = jnp.concatenate(accs, axis=1)
        y_ref[base:base + CONV_ROWS, :] = _ln_swish(y, lg_ref[...], lb_ref[...]).astype(BF16)

    @pl.when(t == pl.num_programs(1) - 1)
    def _():
        hist_ref[...] = buf_ref[CONV_PAD + tc - CONV_HIST:CONV_PAD + tc, :]


def _conv_prompt(u, w, b, lg, lb, tc):
    bsz, t, _ = u.shape
    return pl.pallas_call(
        functools.partial(_conv_prompt_body, tc=tc),
        grid=(bsz, t // tc),
        in_specs=[pl.BlockSpec((None, tc, CONV_CH), lambda bi, ti: (bi, ti, 0)),
                  _const_spec((CONV_WIDTH, CONV_CH)), _const_spec((1, CONV_CH)),
                  _const_spec((1, CONV_CH)), _const_spec((1, CONV_CH))],
        out_specs=[pl.BlockSpec((None, tc, CONV_CH), lambda bi, ti: (bi, ti, 0)),
                   pl.BlockSpec((None, CONV_HIST, CONV_CH), lambda bi, ti: (bi, 0, 0))],
        out_shape=[jax.ShapeDtypeStruct((bsz, t, CONV_CH), BF16),
                   jax.ShapeDtypeStruct((bsz, CONV_HIST, CONV_CH), F32)],
        scratch_shapes=[pltpu.VMEM((CONV_PAD + tc, CONV_CH), F32)],
        compiler_params=pltpu.CompilerParams(dimension_semantics=("arbitrary", "arbitrary"),
                                             vmem_limit_bytes=VMEM_LIMIT),
        name="conv_prompt",
    )(u, w, b, lg, lb)


def _conv_sample_body(u_ref, st_ref, w_ref, b_ref, lg_ref, lb_ref, y_ref, hist_ref, buf_ref, *, ts):
    buf_ref[:, 0:CONV_HIST, :] = st_ref[...]
    buf_ref[:, CONV_HIST:CONV_HIST + ts, :] = u_ref[...]
    nseq = u_ref.shape[0]
    acc = jnp.broadcast_to(b_ref[...][None], (nseq, ts, CONV_CH))
    for j in range(CONV_WIDTH):
        acc = acc + w_ref[j:j + 1, :][None] * buf_ref[:, j:j + ts, :]
    y_ref[...] = _ln_swish(acc, lg_ref[...][None], lb_ref[...][None])
    hist_ref[...] = buf_ref[:, ts:ts + CONV_HIST, :]


def _conv_sample(u, state, w, b, lg, lb):
    nseq, ts, _ = u.shape
    full = lambda s: pl.BlockSpec(s, lambda i: (0,) * len(s))
    return pl.pallas_call(
        functools.partial(_conv_sample_body, ts=ts),
        grid=(1,),
        in_specs=[full(u.shape), full(state.shape), full(w.shape), full(b.shape), full(lg.shape),
                  full(lb.shape)],
        out_specs=[full(u.shape), full(state.shape)],
        out_shape=[jax.ShapeDtypeStruct(u.shape, F32), jax.ShapeDtypeStruct(state.shape, F32)],
        scratch_shapes=[pltpu.VMEM((nseq, CONV_HIST + ts, CONV_CH), F32)],
        compiler_params=pltpu.CompilerParams(dimension_semantics=("arbitrary",),
                                             vmem_limit_bytes=VMEM_LIMIT),
        name="conv_sample",
    )(u, state, w, b, lg, lb)


def _attn_prompt_body(bias_ref, q_ref, k_ref, v_ref, tri_ref, o_ref, qm_ref, run_ref, acc_ref, d_ref,
                      tot_ref):
    hp = pl.program_id(1)
    qs = pl.program_id(2)
    blk = ATT_BLOCK
    tq = blk * ATT_QSUB
    tri = tri_ref[...]
    q2 = q_ref[...]
    lane = lax.broadcasted_iota(jnp.int32, (tq, LANES), 1)
    for h in range(HEADS_PER_LANE_TILE):
        in_head = (lane >= h * HEAD_DIM) & (lane < (h + 1) * HEAD_DIM)
        qm_ref[h * tq:(h + 1) * tq, :] = jnp.where(in_head, q2, jnp.zeros_like(q2))
    run_ref[...] = jnp.zeros_like(run_ref)
    acc_ref[...] = jnp.zeros_like(acc_ref)
    bias = [bias_ref[hp * HEADS_PER_LANE_TILE + h] for h in range(HEADS_PER_LANE_TILE)]

    def rows_of(sub_lo):
        return [slice(h * tq + sub_lo * blk, (h + 1) * tq) for h in range(HEADS_PER_LANE_TILE)]

    def scores(j, sub_lo, causal):
        n = tq - sub_lo * blk
        start = pl.multiple_of(j * blk, blk)
        kj = k_ref[pl.ds(start, blk), :]
        qm = jnp.concatenate([qm_ref[r, :] for r in rows_of(sub_lo)], axis=0)
        z = lax.dot_general(qm, kj, (((1,), (1,)), ((), ())), preferred_element_type=F32)
        z = jnp.concatenate([z[h * n:(h + 1) * n] + bias[h] for h in range(HEADS_PER_LANE_TILE)], axis=0)
        c = _softplus2(z)
        if causal is not None:
            c = jnp.where(causal, c, 0.0)
        later = jnp.dot(c.astype(BF16), tri, preferred_element_type=F32)
        total = jnp.broadcast_to(jnp.sum(c, axis=-1, keepdims=True), (HEADS_PER_LANE_TILE * n, LANES))
        return (z - c) - later, total

    def apply(j, sub_lo, d, total, causal):
        n = tq - sub_lo * blk
        rows = rows_of(sub_lo)
        start = pl.multiple_of(j * blk, blk)
        vj = v_ref[pl.ds(start, blk), :]
        run = jnp.concatenate([run_ref[r, :] for r in rows], axis=0)
        a = _exp2(d - jnp.concatenate([run, run], axis=1))
        if causal is not None:
            a = jnp.where(causal, a, 0.0)
        pv = jnp.dot(a.astype(BF16), vj, preferred_element_type=F32)
        run = run + total
        for h, r in enumerate(rows):
            acc_ref[r, :] += pv[h * n:(h + 1) * n]
            run_ref[r, :] = run[h * n:(h + 1) * n]

    base = qs * ATT_QSUB
    for jj in reversed(range(ATT_QSUB)):
        n = tq - jj * blk
        r = lax.broadcasted_iota(jnp.int32, (n, blk), 0)
        col = lax.broadcasted_iota(jnp.int32, (n, blk), 1)
        causal = col < r
        causal = jnp.concatenate([causal] * HEADS_PER_LANE_TILE, axis=0)
        d, total = scores(base + jj, jj, causal)
        apply(base + jj, jj, d, total, causal)

    def stage(j):
        d, total = scores(j, 0, None)
        d_ref[...] = d
        tot_ref[...] = total

    def drain(j):
        apply(j, 0, d_ref[...], tot_ref[...], None)

    @pl.when(qs > 0)
    def _():
        stage(base - 1)

        def body(i, carry):
            drain(base - 1 - i)
            stage(base - 2 - i)
            return carry

        lax.fori_loop(0, base - 1, body, 0)
        drain(0)

    lane_head0 = lane < HEAD_DIM
    o_ref[...] = jnp.where(lane_head0, acc_ref[0:tq, :], acc_ref[tq:2 * tq, :]).astype(BF16)


def _attn_prompt(bias, q, kb, vb, tri):
    bsz, t, _ = q.shape
    blk = ATT_BLOCK
    tq = blk * ATT_QSUB
    nrow = tq * HEADS_PER_LANE_TILE
    return pl.pallas_call(
        _attn_prompt_body,
        grid=(bsz, N_HEAD_PAIRS, t // tq),
        in_specs=[pl.BlockSpec(memory_space=pltpu.SMEM),
                  pl.BlockSpec((None, tq, LANES), lambda b, h, i: (b, i, h)),
                  pl.BlockSpec((None, t, LANES), lambda b, h, i: (b, 0, h)),
                  pl.BlockSpec((None, t, LANES), lambda b, h, i: (b, 0, h)),
                  _const_spec((blk, blk))],
        out_specs=pl.BlockSpec((None, tq, LANES), lambda b, h, i: (b, i, h)),
        out_shape=jax.ShapeDtypeStruct((bsz, t, ATT_WIDTH), BF16),
        scratch_shapes=[pltpu.VMEM((nrow, LANES), BF16), pltpu.VMEM((nrow, LANES), F32),
                        pltpu.VMEM((nrow, LANES), F32), pltpu.VMEM((nrow, blk), F32),
                        pltpu.VMEM((nrow, LANES), F32)],
        compiler_params=pltpu.CompilerParams(
            dimension_semantics=("arbitrary", "arbitrary", "arbitrary"), vmem_limit_bytes=VMEM_LIMIT),
        name="attn_prompt",
    )(bias, q, kb, vb, tri)


def _attn_sample_body(pt_ref, qbd_ref, biasr_ref, knew_ref, vnew_ref, tri_ref, hm_ref, *rest, ts):
    del pt_ref
    k_refs = rest[:PAGES_PER_STEP]
    v_refs = rest[PAGES_PER_STEP:2 * PAGES_PER_STEP]
    o_ref, run_ref, acc_ref = rest[2 * PAGES_PER_STEP:]
    g = pl.program_id(1)
    qbd = qbd_ref[...]
    biasr = biasr_ref[...]
    tri = tri_ref[...]

    nrow = ts * N_HEADS
    nt = (((1,), (1,)), ((), ()))

    @pl.when(g == 0)
    def _():
        key = lax.broadcasted_iota(jnp.int32, (nrow, PAGE_SIZE), 1)
        qidx = lax.broadcasted_iota(jnp.int32, (nrow, PAGE_SIZE), 0) // N_HEADS
        mask = (key < qidx) & (key < ts)
        z = jnp.dot(qbd, knew_ref[...], preferred_element_type=F32) + biasr
        c = jnp.where(mask, _softplus2(z), 0.0)
        sums = jnp.dot(c.astype(BF16), tri, preferred_element_type=F32)
        a = jnp.where(mask, _exp2((z - c) - sums[:, :PAGE_SIZE]), 0.0)
        acc_ref[...] = lax.dot_general(a, vnew_ref[...], nt, preferred_element_type=F32)
        run_ref[...] = sums[:, PAGE_SIZE:]

    order = list(reversed(range(PAGES_PER_STEP)))
    zs = [jnp.dot(qbd, k_refs[p][...], preferred_element_type=F32) + biasr for p in order]
    cs = [_softplus2(z) for z in zs]
    sums = jnp.dot(jnp.concatenate(cs, axis=0).astype(BF16), tri, preferred_element_type=F32)
    run = run_ref[...]
    acc = acc_ref[...]
    for n, p in enumerate(order):
        rows = slice(n * nrow, (n + 1) * nrow)
        a = _exp2((zs[n] - cs[n]) - sums[rows, :PAGE_SIZE] - run)
        acc = acc + lax.dot_general(a, v_refs[p][...], nt, preferred_element_type=F32)
        run = run + sums[rows, PAGE_SIZE:]
    run_ref[...] = run
    acc_ref[...] = acc

    @pl.when(g == pl.num_programs(1) - 1)
    def _():
        res = acc.reshape(ts, N_HEADS, ATT_WIDTH) * hm_ref[...][None]
        o_ref[...] = jnp.sum(res, axis=1)


def _attn_sample(page_table, qbd, biasr, knew, vnew, tri, hm, cache_kt, cache_vt, layer, ts):
    nseq, n_pages = page_table.shape
    n_steps = n_pages // PAGES_PER_STEP
    nrow = ts * N_HEADS

    def page_spec(p):
        def imap(b, g, pt):
            return (layer, pt[b, (n_steps - 1 - g) * PAGES_PER_STEP + p], 0, 0)
        return pl.BlockSpec((None, None, ATT_WIDTH, PAGE_SIZE), imap)

    per_seq = lambda s: pl.BlockSpec((None,) + s, lambda b, g, pt: (b,) + (0,) * len(s))
    const = lambda s: pl.BlockSpec(s, lambda b, g, pt: (0,) * len(s))
    grid_spec = pltpu.PrefetchScalarGridSpec(
        num_scalar_prefetch=1,
        grid=(nseq, n_steps),
        in_specs=[per_seq((nrow, ATT_WIDTH)), const((nrow, PAGE_SIZE)),
                  per_seq((ATT_WIDTH, PAGE_SIZE)), per_seq((ATT_WIDTH, PAGE_SIZE)),
                  const((PAGE_SIZE, 2 * PAGE_SIZE)), const((N_HEADS, ATT_WIDTH))]
                 + [page_spec(p) for p in range(PAGES_PER_STEP)] * 2,
        out_specs=per_seq((ts, ATT_WIDTH)),
        scratch_shapes=[pltpu.VMEM((nrow, PAGE_SIZE), F32), pltpu.VMEM((nrow, ATT_WIDTH), F32)],
    )
    return pl.pallas_call(
        functools.partial(_attn_sample_body, ts=ts),
        grid_spec=grid_spec,
        out_shape=jax.ShapeDtypeStruct((nseq, ts, ATT_WIDTH), F32),
        compiler_params=pltpu.CompilerParams(dimension_semantics=("arbitrary", "arbitrary"),
                                             vmem_limit_bytes=VMEM_LIMIT),
        name="attn_sample",
    )(page_table, qbd, biasr, knew, vnew, tri, hm,
      *([cache_kt] * PAGES_PER_STEP), *([cache_vt] * PAGES_PER_STEP))


def _mix_body(x_ref, y_ref, att_ref, gpre_ref, gpost_ref, wg_ref, wc_ref, wa_ref, wo_ref, o_ref):
    x = x_ref[...]
    h = _rms(x, gpre_ref[...]).astype(BF16)
    gates = jnp.dot(h, wg_ref[...], preferred_element_type=F32)
    conv_d = jnp.dot(y_ref[...].astype(BF16), wc_ref[...], preferred_element_type=F32)
    att_d = jnp.dot(att_ref[...].astype(BF16), wa_ref[...], preferred_element_type=F32)
    merged = (jax.nn.sigmoid(gates[:, :D_MODEL]) * conv_d
              + jax.nn.sigmoid(gates[:, D_MODEL:]) * att_d)
    mo = jnp.dot(merged.astype(BF16), wo_ref[...], preferred_element_type=F32)
    o_ref[...] = x + _rms(mo, gpost_ref[...])


def _mix(x, y, att, gpre, gpost, wg, wc, wa, wo, tm):
    t = x.shape[0]
    tspec = lambda w: pl.BlockSpec((tm, w), lambda i: (i, 0))
    return pl.pallas_call(
        _mix_body,
        grid=(t // tm,),
        in_specs=[tspec(D_MODEL), tspec(CONV_CH), tspec(ATT_WIDTH),
                  _const_spec((1, D_MODEL)), _const_spec((1, D_MODEL)),
                  _const_spec(wg.shape), _const_spec(wc.shape), _const_spec(wa.shape),
                  _const_spec(wo.shape)],
        out_specs=tspec(D_MODEL),
        out_shape=jax.ShapeDtypeStruct((t, D_MODEL), F32),
        compiler_params=pltpu.CompilerParams(dimension_semantics=("arbitrary",),
                                             vmem_limit_bytes=VMEM_LIMIT),
        name="mix",
    )(x, y, att, gpre, gpost, wg, wc, wa, wo)


def _ffn_body(x_ref, gpre_ref, gpost_ref, w1_ref, w2_ref, o_ref):
    x = x_ref[...]
    h = _rms(x, gpre_ref[...]).astype(BF16)
    hf = jnp.dot(h, w1_ref[...], preferred_element_type=F32)
    a = hf[:, :D_FF]
    ff = (a * jax.nn.sigmoid(a)) * hf[:, D_FF:]
    out = jnp.dot(ff.astype(BF16), w2_ref[...], preferred_element_type=F32)
    o_ref[...] = x + _rms(out, gpost_ref[...])


def _ffn(x, gpre, gpost, w1, w2, tm):
    t = x.shape[0]
    tspec = pl.BlockSpec((tm, D_MODEL), lambda i: (i, 0))
    return pl.pallas_call(
        _ffn_body,
        grid=(t // tm,),
        in_specs=[tspec, _const_spec((1, D_MODEL)), _const_spec((1, D_MODEL)),
                  _const_spec(w1.shape), _const_spec(w2.shape)],
        out_specs=tspec,
        out_shape=jax.ShapeDtypeStruct((t, D_MODEL), F32),
        compiler_params=pltpu.CompilerParams(dimension_semantics=("arbitrary",),
                                             vmem_limit_bytes=VMEM_LIMIT),
        name="ffn",
    )(x, gpre, gpost, w1, w2)


def _strict_upper(n):
    r = lax.broadcasted_iota(jnp.int32, (n, n), 0)
    c = lax.broadcasted_iota(jnp.int32, (n, n), 1)
    return (r > c).astype(BF16)


def kernel(x_prompt, x_sample, cache_k, cache_v, state_conv, page_table, norm_pre_mix, norm_post_mix,
           norm_pre_ffn, norm_post_ffn, w_in, conv_dw, conv_b, conv_ln_g, conv_ln_b, w_conv_out,
           att_bias, w_att_out, w_o, w_ffn_in, w_ffn_out):
    depth = w_in.shape[0]
    bp, tp, _ = x_prompt.shape
    bs, ts, _ = x_sample.shape
    n_phys = cache_k.shape[1]
    tm_p = 512
    tm_s = bs * ts

    hp = x_prompt.reshape(bp * tp, D_MODEL)
    hs = x_sample.reshape(bs * ts, D_MODEL)
    ckt = jnp.transpose(cache_k, (0, 1, 3, 4, 2)).reshape(depth, n_phys, ATT_WIDTH, PAGE_SIZE)
    cvt = jnp.transpose(cache_v, (0, 1, 3, 4, 2)).reshape(depth, n_phys, ATT_WIDTH, PAGE_SIZE)

    tri_p = _strict_upper(ATT_BLOCK)
    tri_s = jnp.concatenate([_strict_upper(PAGE_SIZE), jnp.ones((PAGE_SIZE, PAGE_SIZE), BF16)], axis=1)
    head_of_col = jnp.arange(ATT_WIDTH, dtype=jnp.int32) // HEAD_DIM
    hm = (head_of_col[None, :] == jnp.arange(N_HEADS, dtype=jnp.int32)[:, None]).astype(F32)

    row = lambda a: a.reshape(1, -1)
    kp_l, vp_l, cp_l, ks_l, vs_l, cs_l = [], [], [], [], [], []
    o1 = 2 * CONV_CH
    o2 = o1 + 3 * ATT_WIDTH
    for l in range(depth):
        wglu = w_in[l, :, :o1].astype(BF16)
        wqkv = w_in[l, :, o1:o2].astype(BF16)
        wg = w_in[l, :, o2:].astype(BF16)
        wc = w_conv_out[l].astype(BF16)
        wa = w_att_out[l].astype(BF16)
        wo = w_o[l].astype(BF16)
        w1 = w_ffn_in[l].astype(BF16)
        w2 = w_ffn_out[l].astype(BF16)
        gpre, gpost = row(norm_pre_mix[l]), row(norm_post_mix[l])
        gpre_f, gpost_f = row(norm_pre_ffn[l]), row(norm_post_ffn[l])
        cw, cb = conv_dw[l], row(conv_b[l])
        lg, lb = row(conv_ln_g[l]), row(conv_ln_b[l])
        bias = att_bias[l].astype(F32) * LOG2E

        u, q, k, v, kb, vb = _inproj(hp, gpre, wglu, wqkv, tm_p)
        y, hist = _conv_prompt(u.reshape(bp, tp, CONV_CH), cw, cb, lg, lb, 256)
        att = _attn_prompt(bias, q.reshape(bp, tp, ATT_WIDTH), kb.reshape(bp, tp, ATT_WIDTH),
                           vb.reshape(bp, tp, ATT_WIDTH), tri_p)
        hp = _mix(hp, y.reshape(bp * tp, CONV_CH), att.reshape(bp * tp, ATT_WIDTH),
                  gpre, gpost, wg, wc, wa, wo, tm_p)
        hp = _ffn(hp, gpre_f, gpost_f, w1, w2, 256)
        kp_l.append(k.reshape(bp, tp, N_HEADS, HEAD_DIM))
        vp_l.append(v.reshape(bp, tp, N_HEADS, HEAD_DIM))
        cp_l.append(hist)

        u, q, k, v, _, _ = _inproj(hs, gpre, wglu, wqkv, tm_s)
        y, hist = _conv_sample(u.reshape(bs, ts, CONV_CH), state_conv[l], cw, cb, lg, lb)
        qbd = (q.astype(F32).reshape(bs, ts, 1, ATT_WIDTH) * hm[None, None]).reshape(
            bs, ts * N_HEADS, ATT_WIDTH)
        biasr = jnp.broadcast_to(jnp.tile(bias, ts)[:, None], (ts * N_HEADS, PAGE_SIZE))
        pad_keys = ((0, 0), (0, 0), (0, PAGE_SIZE - ts))
        knew = jnp.pad(jnp.swapaxes(k.reshape(bs, ts, ATT_WIDTH), 1, 2), pad_keys)
        vnew = jnp.pad(jnp.swapaxes(v.reshape(bs, ts, ATT_WIDTH), 1, 2), pad_keys)
        att = _attn_sample(page_table, qbd, biasr, knew, vnew, tri_s, hm, ckt, cvt, l, ts)
        hs = _mix(hs, y.reshape(bs * ts, CONV_CH), att.reshape(bs * ts, ATT_WIDTH),
                  gpre, gpost, wg, wc, wa, wo, tm_s)
        hs = _ffn(hs, gpre_f, gpost_f, w1, w2, tm_s)
        ks_l.append(k.reshape(bs, ts, N_HEADS, HEAD_DIM))
        vs_l.append(v.reshape(bs, ts, N_HEADS, HEAD_DIM))
        cs_l.append(hist)

    return (hp.reshape(bp, tp, D_MODEL), hs.reshape(bs, ts, D_MODEL),
            jnp.stack(kp_l), jnp.stack(vp_l), jnp.stack(cp_l),
            jnp.stack(ks_l), jnp.stack(vs_l), jnp.stack(cs_l))
```

```python
import functools

import jax
import jax.numpy as jnp
from jax import lax
from jax.experimental import pallas as pl
from jax.experimental.pallas import tpu as pltpu

D_MODEL = 1024
N_HEADS = 8
HEAD_DIM = 64
ATT_WIDTH = N_HEADS * HEAD_DIM
CONV_CH = 512
CONV_WIDTH = 31
CONV_HIST = CONV_WIDTH - 1
D_FF = 2816
EPS = 1e-6
PAGE_SIZE = 128
PAST_LEN = 8192
LOG2E = 1.4426950408889634
LN2 = 0.6931471805599453
Q_SCALE = HEAD_DIM ** -0.5 * LOG2E

LANES = 128
SUBLANES = 8
HEADS_PER_LANE_TILE = LANES // HEAD_DIM
N_HEAD_PAIRS = ATT_WIDTH // LANES
ATT_BLOCK = 256
ATT_QSUB = 4
CONV_ROWS = 64
CONV_PAD = 32
PAGES_PER_STEP = 16
VMEM_LIMIT = 56 * 1024 * 1024

F32 = jnp.float32
BF16 = jnp.bfloat16


def _rms(x, g):
    ms = jnp.mean(x * x, axis=-1, keepdims=True)
    return (x * lax.rsqrt(ms + EPS)) * g


_exp2 = jnp.exp2


def _softplus2(z):
    return jnp.maximum(z, 0.0) + jnp.log2(1.0 + jnp.exp2(jnp.abs(z) * (-1.0)))


def _const_spec(shape):
    return pl.BlockSpec(shape, lambda *_: (0,) * len(shape), pipeline_mode=pl.Buffered(1))


def _inproj_body(x_ref, g_ref, wglu_ref, wqkv_ref, u_ref, q_ref, k_ref, v_ref, kb_ref, vb_ref):
    h = _rms(x_ref[...], g_ref[...]).astype(BF16)
    glu = jnp.dot(h, wglu_ref[...], preferred_element_type=F32)
    u_ref[...] = glu[:, :CONV_CH] * jax.nn.sigmoid(glu[:, CONV_CH:])
    qkv = jnp.dot(h, wqkv_ref[...], preferred_element_type=F32)
    q_ref[...] = (qkv[:, :ATT_WIDTH] * Q_SCALE).astype(BF16)
    k = qkv[:, ATT_WIDTH:2 * ATT_WIDTH]
    v = qkv[:, 2 * ATT_WIDTH:]
    k_ref[...] = k
    v_ref[...] = v
    kb_ref[...] = k.astype(BF16)
    vb_ref[...] = v.astype(BF16)


def _inproj(x, g, wglu, wqkv, tm):
    t = x.shape[0]
    tok = lambda w, dt: jax.ShapeDtypeStruct((t, w), dt)
    tspec = lambda w: pl.BlockSpec((tm, w), lambda i: (i, 0))
    return pl.pallas_call(
        _inproj_body,
        grid=(t // tm,),
        in_specs=[tspec(D_MODEL), _const_spec((1, D_MODEL)),
                  _const_spec(wglu.shape), _const_spec(wqkv.shape)],
        out_specs=[tspec(CONV_CH), tspec(ATT_WIDTH), tspec(ATT_WIDTH), tspec(ATT_WIDTH),
                   tspec(ATT_WIDTH), tspec(ATT_WIDTH)],
        out_shape=[tok(CONV_CH, F32), tok(ATT_WIDTH, BF16), tok(ATT_WIDTH, F32), tok(ATT_WIDTH, F32),
                   tok(ATT_WIDTH, BF16), tok(ATT_WIDTH, BF16)],
        compiler_params=pltpu.CompilerParams(dimension_semantics=("arbitrary",),
                                             vmem_limit_bytes=VMEM_LIMIT),
        name="inproj",
    )(x, g, wglu, wqkv)


def _inproj_prompt_body(*refs, aliased, layer):
    x_ref, g_ref, wglu_ref, wqkv_ref = refs[:4]
    u_ref, q_ref, kt_ref, vt_ref, kb_ref, vb_ref = refs[4 + 2 * aliased:]
    h = _rms(x_ref[...], g_ref[...]).astype(BF16)
    glu = jnp.dot(h, wglu_ref[...], preferred_element_type=F32)
    u_ref[...] = glu[:, :CONV_CH] * jax.nn.sigmoid(glu[:, CONV_CH:])
    qkv = jnp.dot(h, wqkv_ref[...], preferred_element_type=F32)
    q_ref[...] = (qkv[:, :ATT_WIDTH] * Q_SCALE).astype(BF16)
    k = qkv[:, ATT_WIDTH:2 * ATT_WIDTH]
    v = qkv[:, 2 * ATT_WIDTH:]
    if aliased:
        kt_ref[...] = k.T
        vt_ref[...] = v.T
    else:
        for l in range(kt_ref.shape[0]):
            kt_ref[l] = k.T if l == layer else jnp.zeros(kt_ref.shape[1:], F32)
            vt_ref[l] = v.T if l == layer else jnp.zeros(vt_ref.shape[1:], F32)
    kb_ref[...] = k.astype(BF16)
    vb_ref[...] = v.astype(BF16)


def _inproj_prompt(x, g, wglu, wqkv, tm, layer, depth, bsz, kt_all, vt_all):
    t = x.shape[0]
    per_seq = t // bsz // tm
    aliased = kt_all is not None
    tok = lambda w, dt: jax.ShapeDtypeStruct((t, w), dt)
    tspec = lambda w: pl.BlockSpec((tm, w), lambda i: (i, 0))
    stacked = jax.ShapeDtypeStruct((depth, bsz, ATT_WIDTH, t // bsz), F32)
    if aliased:
        tspec_t = pl.BlockSpec((None, None, ATT_WIDTH, tm),
                               lambda i: (layer, i // per_seq, 0, i % per_seq))
    else:
        tspec_t = pl.BlockSpec((depth, None, ATT_WIDTH, tm), lambda i: (0, i // per_seq, 0, i % per_seq))
    in_specs = [tspec(D_MODEL), _const_spec((1, D_MODEL)), _const_spec(wglu.shape), _const_spec(wqkv.shape)]
    args = [x, g, wglu, wqkv]
    if aliased:
        in_specs += [pl.BlockSpec(memory_space=pl.ANY)] * 2
        args += [kt_all, vt_all]
    return pl.pallas_call(
        functools.partial(_inproj_prompt_body, aliased=aliased, layer=layer),
        grid=(t // tm,),
        in_specs=in_specs,
        out_specs=[tspec(CONV_CH), tspec(ATT_WIDTH), tspec_t, tspec_t, tspec(ATT_WIDTH), tspec(ATT_WIDTH)],
        out_shape=[tok(CONV_CH, F32), tok(ATT_WIDTH, BF16), stacked, stacked,
                   tok(ATT_WIDTH, BF16), tok(ATT_WIDTH, BF16)],
        input_output_aliases={4: 2, 5: 3} if aliased else {},
        compiler_params=pltpu.CompilerParams(dimension_semantics=("arbitrary",),
                                             vmem_limit_bytes=VMEM_LIMIT),
        name="inproj_prompt",
    )(*args)


def _ln_swish(y, lg, lb):
    mu = jnp.mean(y, axis=-1, keepdims=True)
    d = y - mu
    var = jnp.mean(d * d, axis=-1, keepdims=True)
    yn = (d * lax.rsqrt(var + EPS)) * lg + lb
    return yn * jax.nn.sigmoid(yn)


def _conv_tile(t, u_ref, w_ref, b_ref, lg_ref, lb_ref, y_ref, buf_ref, tc):
    @pl.when(t == 0)
    def _():
        buf_ref[0:CONV_PAD, :] = jnp.zeros((CONV_PAD, CONV_CH), F32)

    @pl.when(t > 0)
    def _():
        buf_ref[0:CONV_PAD, :] = buf_ref[tc:tc + CONV_PAD, :]

    buf_ref[CONV_PAD:CONV_PAD + tc, :] = u_ref[...]
    first_tap = CONV_PAD - CONV_HIST

    shifts = [first_tap + j for j in range(CONV_WIDTH)]
    by_rho = {rho: [s for s in shifts if s % SUBLANES == rho] for rho in range(SUBLANES)}
    win = max(shifts) + CONV_ROWS
    for base in range(0, tc, CONV_ROWS):
        accs = []
        for lt in range(CONV_CH // LANES):
            cols = slice(lt * LANES, (lt + 1) * LANES)
            window = buf_ref[base:base + win, cols]
            acc = jnp.broadcast_to(b_ref[:, cols], (CONV_ROWS, LANES))
            for rho, group in by_rho.items():
                if not group:
                    continue
                shifted = pltpu.roll(window, win - rho, 0) if rho else window
                for s in group:
                    j = s - first_tap
                    acc = acc + w_ref[j:j + 1, cols] * shifted[s - rho:s - rho + CONV_ROWS, :]
            accs.append(acc)
        y = jnp.concatenate(accs, axis=1)
        y_ref[base:base + CONV_ROWS, :] = _ln_swish(y, lg_ref[...], lb_ref[...]).astype(BF16)


def _conv_sample_body(u_ref, st_ref, w_ref, b_ref, lg_ref, lb_ref, y_ref, hist_ref, buf_ref, *, ts):
    buf_ref[:, 0:CONV_HIST, :] = st_ref[...]
    buf_ref[:, CONV_HIST:CONV_HIST + ts, :] = u_ref[...]
    nseq = u_ref.shape[0]
    acc = jnp.broadcast_to(b_ref[...][None], (nseq, ts, CONV_CH))
    for j in range(CONV_WIDTH):
        acc = acc + w_ref[j:j + 1, :][None] * buf_ref[:, j:j + ts, :]
    y_ref[...] = _ln_swish(acc, lg_ref[...][None], lb_ref[...][None])
    hist_ref[...] = buf_ref[:, ts:ts + CONV_HIST, :]


def _conv_sample(u, state, w, b, lg, lb):
    nseq, ts, _ = u.shape
    full = lambda s: pl.BlockSpec(s, lambda i: (0,) * len(s))
    return pl.pallas_call(
        functools.partial(_conv_sample_body, ts=ts),
        grid=(1,),
        in_specs=[full(u.shape), full(state.shape), full(w.shape), full(b.shape), full(lg.shape),
                  full(lb.shape)],
        out_specs=[full(u.shape), full(state.shape)],
        out_shape=[jax.ShapeDtypeStruct(u.shape, F32), jax.ShapeDtypeStruct(state.shape, F32)],
        scratch_shapes=[pltpu.VMEM((nseq, CONV_HIST + ts, CONV_CH), F32)],
        compiler_params=pltpu.CompilerParams(dimension_semantics=("arbitrary",),
                                             vmem_limit_bytes=VMEM_LIMIT),
        name="conv_sample",
    )(u, state, w, b, lg, lb)


def _attn_prompt_body(bias_ref, q_ref, k_ref, v_ref, tri_ref, o_ref, qm_ref, run_ref, acc_ref, d_ref,
                      tot_ref):
    hp = pl.program_id(1)
    qs = pl.program_id(2)
    blk = ATT_BLOCK
    tq = blk * ATT_QSUB
    tri = tri_ref[...]
    q2 = q_ref[...]
    lane = lax.broadcasted_iota(jnp.int32, (tq, LANES), 1)
    for h in range(HEADS_PER_LANE_TILE):
        in_head = (lane >= h * HEAD_DIM) & (lane < (h + 1) * HEAD_DIM)
        qm_ref[h * tq:(h + 1) * tq, :] = jnp.where(in_head, q2, jnp.zeros_like(q2))
    run_ref[...] = jnp.zeros_like(run_ref)
    acc_ref[...] = jnp.zeros_like(acc_ref)
    bias = [bias_ref[hp * HEADS_PER_LANE_TILE + h] for h in range(HEADS_PER_LANE_TILE)]

    def rows_of(sub_lo):
        return [slice(h * tq + sub_lo * blk, (h + 1) * tq) for h in range(HEADS_PER_LANE_TILE)]

    def scores(j, sub_lo, causal):
        n = tq - sub_lo * blk
        start = pl.multiple_of(j * blk, blk)
        kj = k_ref[pl.ds(start, blk), :]
        qm = jnp.concatenate([qm_ref[r, :] for r in rows_of(sub_lo)], axis=0)
        z = lax.dot_general(qm, kj, (((1,), (1,)), ((), ())), preferred_element_type=F32)
        z = jnp.concatenate([z[h * n:(h + 1) * n] + bias[h] for h in range(HEADS_PER_LANE_TILE)], axis=0)
        c = _softplus2(z)
        if causal is not None:
            c = jnp.where(causal, c, 0.0)
        later = jnp.dot(c.astype(BF16), tri, preferred_element_type=F32)
        total = jnp.broadcast_to(jnp.sum(c, axis=-1, keepdims=True), (HEADS_PER_LANE_TILE * n, LANES))
        return (z - c) - later, total

    def apply(j, sub_lo, d, total, causal):
        n = tq - sub_lo * blk
        rows = rows_of(sub_lo)
        start = pl.multiple_of(j * blk, blk)
        vj = v_ref[pl.ds(start, blk), :]
        run = jnp.concatenate([run_ref[r, :] for r in rows], axis=0)
        a = _exp2(d - jnp.concatenate([run, run], axis=1))
        if causal is not None:
            a = jnp.where(causal, a, 0.0)
        pv = jnp.dot(a.astype(BF16), vj, preferred_element_type=F32)
        run = run + total
        for h, r in enumerate(rows):
            acc_ref[r, :] += pv[h * n:(h + 1) * n]
            run_ref[r, :] = run[h * n:(h + 1) * n]

    base = qs * ATT_QSUB
    for jj in reversed(range(ATT_QSUB)):
        n = tq - jj * blk
        r = lax.broadcasted_iota(jnp.int32, (n, blk), 0)
        col = lax.broadcasted_iota(jnp.int32, (n, blk), 1)
        causal = col < r
        causal = jnp.concatenate([causal] * HEADS_PER_LANE_TILE, axis=0)
        d, total = scores(base + jj, jj, causal)
        apply(base + jj, jj, d, total, causal)

    def stage(j):
        d, total = scores(j, 0, None)
        d_ref[...] = d
        tot_ref[...] = total

    def drain(j):
        apply(j, 0, d_ref[...], tot_ref[...], None)

    @pl.when(qs > 0)
    def _():
        stage(base - 1)

        def body(i, carry):
            drain(base - 1 - i)
            stage(base - 2 - i)
            return carry

        lax.fori_loop(0, base - 1, body, 0)
        drain(0)

    lane_head0 = lane < HEAD_DIM
    o_ref[...] = jnp.where(lane_head0, acc_ref[0:tq, :], acc_ref[tq:2 * tq, :]).astype(BF16)


def _attn_prompt(bias, q, kb, vb, tri):
    bsz, t, _ = q.shape
    blk = ATT_BLOCK
    tq = blk * ATT_QSUB
    nrow = tq * HEADS_PER_LANE_TILE
    return pl.pallas_call(
        _attn_prompt_body,
        grid=(bsz, N_HEAD_PAIRS, t // tq),
        in_specs=[pl.BlockSpec(memory_space=pltpu.SMEM),
                  pl.BlockSpec((None, tq, LANES), lambda b, h, i: (b, i, h)),
                  pl.BlockSpec((None, t, LANES), lambda b, h, i: (b, 0, h)),
                  pl.BlockSpec((None, t, LANES), lambda b, h, i: (b, 0, h)),
                  _const_spec((blk, blk))],
        out_specs=pl.BlockSpec((None, tq, LANES), lambda b, h, i: (b, i, h)),
        out_shape=jax.ShapeDtypeStruct((bsz, t, ATT_WIDTH), BF16),
        scratch_shapes=[pltpu.VMEM((nrow, LANES), BF16), pltpu.VMEM((nrow, LANES), F32),
                        pltpu.VMEM((nrow, LANES), F32), pltpu.VMEM((nrow, blk), F32),
                        pltpu.VMEM((nrow, LANES), F32)],
        compiler_params=pltpu.CompilerParams(
            dimension_semantics=("arbitrary", "arbitrary", "arbitrary"), vmem_limit_bytes=VMEM_LIMIT),
        name="attn_prompt",
    )(bias, q, kb, vb, tri)


def _attn_sample_body(pt_ref, qbd_ref, biasr_ref, knew_ref, vnew_ref, tri_ref, hm_ref, *rest, ts):
    del pt_ref
    k_refs = rest[:PAGES_PER_STEP]
    v_refs = rest[PAGES_PER_STEP:2 * PAGES_PER_STEP]
    o_ref, run_ref, acc_ref = rest[2 * PAGES_PER_STEP:]
    g = pl.program_id(1)
    qbd = qbd_ref[...]
    biasr = biasr_ref[...]
    tri = tri_ref[...]

    nrow = ts * N_HEADS
    nt = (((1,), (1,)), ((), ()))

    @pl.when(g == 0)
    def _():
        key = lax.broadcasted_iota(jnp.int32, (nrow, PAGE_SIZE), 1)
        qidx = lax.broadcasted_iota(jnp.int32, (nrow, PAGE_SIZE), 0) // N_HEADS
        mask = (key < qidx) & (key < ts)
        z = jnp.dot(qbd, knew_ref[...], preferred_element_type=F32) + biasr
        c = jnp.where(mask, _softplus2(z), 0.0)
        sums = jnp.dot(c.astype(BF16), tri, preferred_element_type=F32)
        a = jnp.where(mask, _exp2((z - c) - sums[:, :PAGE_SIZE]), 0.0)
        acc_ref[...] = lax.dot_general(a, vnew_ref[...], nt, preferred_element_type=F32)
        run_ref[...] = sums[:, PAGE_SIZE:]

    order = list(reversed(range(PAGES_PER_STEP)))
    zs = [jnp.dot(qbd, k_refs[p][...], preferred_element_type=F32) + biasr for p in order]
    cs = [_softplus2(z) for z in zs]
    sums = jnp.dot(jnp.concatenate(cs, axis=0).astype(BF16), tri, preferred_element_type=F32)
    run = run_ref[...]
    acc = acc_ref[...]
    for n, p in enumerate(order):
        rows = slice(n * nrow, (n + 1) * nrow)
        a = _exp2((zs[n] - cs[n]) - sums[rows, :PAGE_SIZE] - run)
        acc = acc + lax.dot_general(a, v_refs[p][...], nt, preferred_element_type=F32)
        run = run + sums[rows, PAGE_SIZE:]
    run_ref[...] = run
    acc_ref[...] = acc

    @pl.when(g == pl.num_programs(1) - 1)
    def _():
        res = acc.reshape(ts, N_HEADS, ATT_WIDTH) * hm_ref[...][None]
        o_ref[...] = jnp.sum(res, axis=1)


def _attn_sample(page_table, qbd, biasr, knew, vnew, tri, hm, cache_kt, cache_vt, layer, ts):
    nseq, n_pages = page_table.shape
    n_steps = n_pages // PAGES_PER_STEP
    nrow = ts * N_HEADS

    def page_spec(p):
        def imap(b, g, pt):
            return (layer, pt[b, (n_steps - 1 - g) * PAGES_PER_STEP + p], 0, 0)
        return pl.BlockSpec((None, None, ATT_WIDTH, PAGE_SIZE), imap)

    per_seq = lambda s: pl.BlockSpec((None,) + s, lambda b, g, pt: (b,) + (0,) * len(s))
    const = lambda s: pl.BlockSpec(s, lambda b, g, pt: (0,) * len(s))
    grid_spec = pltpu.PrefetchScalarGridSpec(
        num_scalar_prefetch=1,
        grid=(nseq, n_steps),
        in_specs=[per_seq((nrow, ATT_WIDTH)), const((nrow, PAGE_SIZE)),
                  per_seq((ATT_WIDTH, PAGE_SIZE)), per_seq((ATT_WIDTH, PAGE_SIZE)),
                  const((PAGE_SIZE, 2 * PAGE_SIZE)), const((N_HEADS, ATT_WIDTH))]
                 + [page_spec(p) for p in range(PAGES_PER_STEP)] * 2,
        out_specs=per_seq((ts, ATT_WIDTH)),
        scratch_shapes=[pltpu.VMEM((nrow, PAGE_SIZE), F32), pltpu.VMEM((nrow, ATT_WIDTH), F32)],
    )
    return pl.pallas_call(
        functools.partial(_attn_sample_body, ts=ts),
        grid_spec=grid_spec,
        out_shape=jax.ShapeDtypeStruct((nseq, ts, ATT_WIDTH), F32),
        compiler_params=pltpu.CompilerParams(dimension_semantics=("arbitrary", "arbitrary"),
                                             vmem_limit_bytes=VMEM_LIMIT),
        name="attn_sample",
    )(page_table, qbd, biasr, knew, vnew, tri, hm,
      *([cache_kt] * PAGES_PER_STEP), *([cache_vt] * PAGES_PER_STEP))


def _mix_body(x_ref, y_ref, att_ref, gpre_ref, gpost_ref, wg_ref, wc_ref, wa_ref, wo_ref, o_ref):
    x = x_ref[...]
    h = _rms(x, gpre_ref[...]).astype(BF16)
    gates = jnp.dot(h, wg_ref[...], preferred_element_type=F32)
    conv_d = jnp.dot(y_ref[...].astype(BF16), wc_ref[...], preferred_element_type=F32)
    att_d = jnp.dot(att_ref[...].astype(BF16), wa_ref[...], preferred_element_type=F32)
    merged = (jax.nn.sigmoid(gates[:, :D_MODEL]) * conv_d
              + jax.nn.sigmoid(gates[:, D_MODEL:]) * att_d)
    mo = jnp.dot(merged.astype(BF16), wo_ref[...], preferred_element_type=F32)
    o_ref[...] = x + _rms(mo, gpost_ref[...])


def _mix(x, y, att, gpre, gpost, wg, wc, wa, wo, tm):
    t = x.shape[0]
    tspec = lambda w: pl.BlockSpec((tm, w), lambda i: (i, 0))
    return pl.pallas_call(
        _mix_body,
        grid=(t // tm,),
        in_specs=[tspec(D_MODEL), tspec(CONV_CH), tspec(ATT_WIDTH),
                  _const_spec((1, D_MODEL)), _const_spec((1, D_MODEL)),
                  _const_spec(wg.shape), _const_spec(wc.shape), _const_spec(wa.shape),
                  _const_spec(wo.shape)],
        out_specs=tspec(D_MODEL),
        out_shape=jax.ShapeDtypeStruct((t, D_MODEL), F32),
        compiler_params=pltpu.CompilerParams(dimension_semantics=("arbitrary",),
                                             vmem_limit_bytes=VMEM_LIMIT),
        name="mix",
    )(x, y, att, gpre, gpost, wg, wc, wa, wo)


def _mix_conv_body(x_ref, u_ref, att_ref, gpre_ref, gpost_ref, wg_ref, wc_ref, wa_ref, wo_ref,
                   cw_ref, cb_ref, lg_ref, lb_ref, o_ref, buf_ref, y_ref, *, tm, per_seq):
    _conv_tile(pl.program_id(0) % per_seq, u_ref, cw_ref, cb_ref, lg_ref, lb_ref, y_ref, buf_ref, tm)
    _mix_body(x_ref, y_ref, att_ref, gpre_ref, gpost_ref, wg_ref, wc_ref, wa_ref, wo_ref, o_ref)


def _mix_conv(x, u, att, gpre, gpost, wg, wc, wa, wo, cw, cb, lg, lb, tm, bsz):
    t = x.shape[0]
    per_seq = t // bsz // tm
    tspec = lambda w: pl.BlockSpec((tm, w), lambda i: (i, 0))
    return pl.pallas_call(
        functools.partial(_mix_conv_body, tm=tm, per_seq=per_seq),
        grid=(t // tm,),
        in_specs=[tspec(D_MODEL), tspec(CONV_CH), tspec(ATT_WIDTH),
                  _const_spec((1, D_MODEL)), _const_spec((1, D_MODEL)),
                  _const_spec(wg.shape), _const_spec(wc.shape), _const_spec(wa.shape),
                  _const_spec(wo.shape), _const_spec((CONV_WIDTH, CONV_CH)), _const_spec((1, CONV_CH)),
                  _const_spec((1, CONV_CH)), _const_spec((1, CONV_CH))],
        out_specs=tspec(D_MODEL),
        out_shape=jax.ShapeDtypeStruct((t, D_MODEL), F32),
        scratch_shapes=[pltpu.VMEM((CONV_PAD + tm, CONV_CH), F32), pltpu.VMEM((tm, CONV_CH), BF16)],
        compiler_params=pltpu.CompilerParams(dimension_semantics=("arbitrary",),
                                             vmem_limit_bytes=VMEM_LIMIT),
        name="mix_conv",
    )(x, u, att, gpre, gpost, wg, wc, wa, wo, cw, cb, lg, lb)


def _ffn_body(x_ref, gpre_ref, gpost_ref, w1_ref, w2_ref, o_ref):
    x = x_ref[...]
    h = _rms(x, gpre_ref[...]).astype(BF16)
    hf = jnp.dot(h, w1_ref[...], preferred_element_type=F32)
    a = hf[:, :D_FF]
    ff = (a * jax.nn.sigmoid(a)) * hf[:, D_FF:]
    out = jnp.dot(ff.astype(BF16), w2_ref[...], preferred_element_type=F32)
    o_ref[...] = x + _rms(out, gpost_ref[...])


def _ffn(x, gpre, gpost, w1, w2, tm):
    t = x.shape[0]
    tspec = pl.BlockSpec((tm, D_MODEL), lambda i: (i, 0))
    return pl.pallas_call(
        _ffn_body,
        grid=(t // tm,),
        in_specs=[tspec, _const_spec((1, D_MODEL)), _const_spec((1, D_MODEL)),
                  _const_spec(w1.shape), _const_spec(w2.shape)],
        out_specs=tspec,
        out_shape=jax.ShapeDtypeStruct((t, D_MODEL), F32),
        compiler_params=pltpu.CompilerParams(dimension_semantics=("arbitrary",),
                                             vmem_limit_bytes=VMEM_LIMIT),
        name="ffn",
    )(x, gpre, gpost, w1, w2)


def _strict_upper(n):
    r = lax.broadcasted_iota(jnp.int32, (n, n), 0)
    c = lax.broadcasted_iota(jnp.int32, (n, n), 1)
    return (r > c).astype(BF16)


def kernel(x_prompt, x_sample, cache_k, cache_v, state_conv, page_table, norm_pre_mix, norm_post_mix,
           norm_pre_ffn, norm_post_ffn, w_in, conv_dw, conv_b, conv_ln_g, conv_ln_b, w_conv_out,
           att_bias, w_att_out, w_o, w_ffn_in, w_ffn_out):
    depth = w_in.shape[0]
    bp, tp, _ = x_prompt.shape
    bs, ts, _ = x_sample.shape
    n_phys = cache_k.shape[1]
    tm_p = 512
    tm_s = bs * ts

    hp = x_prompt.reshape(bp * tp, D_MODEL)
    hs = x_sample.reshape(bs * ts, D_MODEL)
    ckt = jnp.transpose(cache_k, (0, 1, 3, 4, 2)).reshape(depth, n_phys, ATT_WIDTH, PAGE_SIZE)
    cvt = jnp.transpose(cache_v, (0, 1, 3, 4, 2)).reshape(depth, n_phys, ATT_WIDTH, PAGE_SIZE)

    tri_p = _strict_upper(ATT_BLOCK)
    tri_s = jnp.concatenate([_strict_upper(PAGE_SIZE), jnp.ones((PAGE_SIZE, PAGE_SIZE), BF16)], axis=1)
    head_of_col = jnp.arange(ATT_WIDTH, dtype=jnp.int32) // HEAD_DIM
    hm = (head_of_col[None, :] == jnp.arange(N_HEADS, dtype=jnp.int32)[:, None]).astype(F32)

    row = lambda a: a.reshape(1, -1)
    cp_l, ks_l, vs_l, cs_l = [], [], [], []
    kt_all = vt_all = None
    o1 = 2 * CONV_CH
    o2 = o1 + 3 * ATT_WIDTH
    for l in range(depth):
        wglu = w_in[l, :, :o1].astype(BF16)
        wqkv = w_in[l, :, o1:o2].astype(BF16)
        wg = w_in[l, :, o2:].astype(BF16)
        wc = w_conv_out[l].astype(BF16)
        wa = w_att_out[l].astype(BF16)
        wo = w_o[l].astype(BF16)
        w1 = w_ffn_in[l].astype(BF16)
        w2 = w_ffn_out[l].astype(BF16)
        gpre, gpost = row(norm_pre_mix[l]), row(norm_post_mix[l])
        gpre_f, gpost_f = row(norm_pre_ffn[l]), row(norm_post_ffn[l])
        cw, cb = conv_dw[l], row(conv_b[l])
        lg, lb = row(conv_ln_g[l]), row(conv_ln_b[l])
        bias = att_bias[l].astype(F32) * LOG2E

        u, q, kt_all, vt_all, kb, vb = _inproj_prompt(hp, gpre, wglu, wqkv, tm_p, l, depth, bp,
                                                      kt_all, vt_all)
        att = _attn_prompt(bias, q.reshape(bp, tp, ATT_WIDTH), kb.reshape(bp, tp, ATT_WIDTH),
                           vb.reshape(bp, tp, ATT_WIDTH), tri_p)
        hp = _mix_conv(hp, u, att.reshape(bp * tp, ATT_WIDTH), gpre, gpost, wg, wc, wa, wo,
                       cw, cb, lg, lb, tm_p, bp)
        hist = u.reshape(bp, tp, CONV_CH)[:, tp - CONV_HIST:, :]
        hp = _ffn(hp, gpre_f, gpost_f, w1, w2, 256)
        cp_l.append(hist)

        u, q, k, v, _, _ = _inproj(hs, gpre, wglu, wqkv, tm_s)
        y, hist = _conv_sample(u.reshape(bs, ts, CONV_CH), state_conv[l], cw, cb, lg, lb)
        qbd = (q.astype(F32).reshape(bs, ts, 1, ATT_WIDTH) * hm[None, None]).reshape(
            bs, ts * N_HEADS, ATT_WIDTH)
        biasr = jnp.broadcast_to(jnp.tile(bias, ts)[:, None], (ts * N_HEADS, PAGE_SIZE))
        pad_keys = ((0, 0), (0, 0), (0, PAGE_SIZE - ts))
        knew = jnp.pad(jnp.swapaxes(k.reshape(bs, ts, ATT_WIDTH), 1, 2), pad_keys)
        vnew = jnp.pad(jnp.swapaxes(v.reshape(bs, ts, ATT_WIDTH), 1, 2), pad_keys)
        att = _attn_sample(page_table, qbd, biasr, knew, vnew, tri_s, hm, ckt, cvt, l, ts)
        hs = _mix(hs, y.reshape(bs * ts, CONV_CH), att.reshape(bs * ts, ATT_WIDTH),
                  gpre, gpost, wg, wc, wa, wo, tm_s)
        hs = _ffn(hs, gpre_f, gpost_f, w1, w2, tm_s)
        ks_l.append(k.reshape(bs, ts, N_HEADS, HEAD_DIM))
        vs_l.append(v.reshape(bs, ts, N_HEADS, HEAD_DIM))
        cs_l.append(hist)

    unstack = lambda a: jnp.transpose(a.reshape(depth, bp, N_HEADS, HEAD_DIM, tp), (0, 1, 4, 2, 3))
    return (hp.reshape(bp, tp, D_MODEL), hs.reshape(bs, ts, D_MODEL),
            unstack(kt_all), unstack(vt_all), jnp.stack(cp_l),
            jnp.stack(ks_l), jnp.stack(vs_l), jnp.stack(cs_l))
```

```python
import functools

import jax
import jax.numpy as jnp
from jax import lax
from jax.experimental import pallas as pl
from jax.experimental.pallas import tpu as pltpu

D_MODEL = 1024
N_HEADS = 8
HEAD_DIM = 64
ATT_WIDTH = N_HEADS * HEAD_DIM
CONV_CH = 512
CONV_WIDTH = 31
CONV_HIST = CONV_WIDTH - 1
D_FF = 2816
EPS = 1e-6
PAGE_SIZE = 128
PAST_LEN = 8192
LOG2E = 1.4426950408889634
Q_SCALE = HEAD_DIM ** -0.5 * LOG2E

LANES = 128
SUBLANES = 8
HEADS_PER_LANE_TILE = LANES // HEAD_DIM
N_HEAD_PAIRS = ATT_WIDTH // LANES
ATT_BLOCK = 256
ATT_QSUB = 4
BIAS_PARTS = 2
CONV_ROWS = 64
CONV_PAD = 32
PAGES_PER_STEP = 32
VMEM_LIMIT = 56 * 1024 * 1024

F32 = jnp.float32
BF16 = jnp.bfloat16


def _rms(x, g):
    ms = jnp.mean(x * x, axis=-1, keepdims=True)
    return (x * lax.rsqrt(ms + EPS)) * g


_exp2 = jnp.exp2


def _softplus2(z):
    return jnp.maximum(z, 0.0) + jnp.log2(1.0 + jnp.exp2(-jnp.abs(z)))


def _const_spec(shape):
    return pl.BlockSpec(shape, lambda *_: (0,) * len(shape), pipeline_mode=pl.Buffered(1))


def _project(x_ref, g_ref, wglu_ref, wq_ref, wk_ref, wv_ref):
    h = _rms(x_ref[...], g_ref[...]).astype(BF16)
    dot = lambda w_ref: jnp.dot(h, w_ref[...].astype(BF16), preferred_element_type=F32)
    glu = dot(wglu_ref)
    u = glu[:, :CONV_CH] * jax.nn.sigmoid(glu[:, CONV_CH:])
    q = (dot(wq_ref) * Q_SCALE).astype(BF16)
    return u, q, dot(wk_ref), dot(wv_ref)


def _w_in_specs(layer):
    glu = pl.BlockSpec((None, D_MODEL, 2 * CONV_CH), lambda *_: (layer, 0, 0), pipeline_mode=pl.Buffered(1))
    first = 2 * CONV_CH // ATT_WIDTH
    qkv = [pl.BlockSpec((None, D_MODEL, ATT_WIDTH), lambda *_, n=n: (layer, 0, first + n),
                        pipeline_mode=pl.Buffered(1)) for n in range(3)]
    return [glu] + qkv


def _inproj_body(x_ref, g_ref, wglu_ref, wq_ref, wk_ref, wv_ref, u_ref, q_ref, k_ref, v_ref):
    u_ref[...], q_ref[...], k_ref[...], v_ref[...] = _project(x_ref, g_ref, wglu_ref, wq_ref, wk_ref,
                                                             wv_ref)


def _inproj(x, g, w_in, layer, tm):
    t = x.shape[0]
    tok = lambda w, dt: jax.ShapeDtypeStruct((t, w), dt)
    tspec = lambda w: pl.BlockSpec((tm, w), lambda i: (i, 0))
    return pl.pallas_call(
        _inproj_body,
        grid=(t // tm,),
        in_specs=[tspec(D_MODEL), _const_spec((1, D_MODEL))] + _w_in_specs(layer),
        out_specs=[tspec(CONV_CH), tspec(ATT_WIDTH), tspec(ATT_WIDTH), tspec(ATT_WIDTH)],
        out_shape=[tok(CONV_CH, F32), tok(ATT_WIDTH, BF16), tok(ATT_WIDTH, F32), tok(ATT_WIDTH, F32)],
        compiler_params=pltpu.CompilerParams(dimension_semantics=("arbitrary",),
                                             vmem_limit_bytes=VMEM_LIMIT),
        name="inproj",
    )(x, g, w_in, w_in, w_in, w_in)


def _inproj_prompt_body(*refs, aliased, layer):
    u_ref, q_ref, kt_ref, vt_ref, kb_ref, vb_ref = refs[6 + 2 * aliased:]
    u_ref[...], q_ref[...], k, v = _project(*refs[:6])
    if aliased:
        kt_ref[...] = k.T
        vt_ref[...] = v.T
    else:
        for l in range(kt_ref.shape[0]):
            kt_ref[l] = k.T if l == layer else jnp.zeros(kt_ref.shape[1:], F32)
            vt_ref[l] = v.T if l == layer else jnp.zeros(vt_ref.shape[1:], F32)
    kb_ref[...] = k.astype(BF16)
    vb_ref[...] = v.astype(BF16)


def _inproj_prompt(x, g, w_in, tm, layer, depth, bsz, kt_all, vt_all):
    t = x.shape[0]
    per_seq = t // bsz // tm
    aliased = kt_all is not None
    tok = lambda w, dt: jax.ShapeDtypeStruct((t, w), dt)
    tspec = lambda w: pl.BlockSpec((tm, w), lambda i: (i, 0))
    stacked = jax.ShapeDtypeStruct((depth, bsz, ATT_WIDTH, t // bsz), F32)
    if aliased:
        tspec_t = pl.BlockSpec((None, None, ATT_WIDTH, tm),
                               lambda i: (layer, i // per_seq, 0, i % per_seq))
    else:
        tspec_t = pl.BlockSpec((depth, None, ATT_WIDTH, tm), lambda i: (0, i // per_seq, 0, i % per_seq))
    in_specs = [tspec(D_MODEL), _const_spec((1, D_MODEL))] + _w_in_specs(layer)
    args = [x, g, w_in, w_in, w_in, w_in]
    if aliased:
        in_specs += [pl.BlockSpec(memory_space=pl.ANY)] * 2
        args += [kt_all, vt_all]
    n_in = len(args)
    return pl.pallas_call(
        functools.partial(_inproj_prompt_body, aliased=aliased, layer=layer),
        grid=(t // tm,),
        in_specs=in_specs,
        out_specs=[tspec(CONV_CH), tspec(ATT_WIDTH), tspec_t, tspec_t, tspec(ATT_WIDTH), tspec(ATT_WIDTH)],
        out_shape=[tok(CONV_CH, F32), tok(ATT_WIDTH, BF16), stacked, stacked,
                   tok(ATT_WIDTH, BF16), tok(ATT_WIDTH, BF16)],
        input_output_aliases={n_in - 2: 2, n_in - 1: 3} if aliased else {},
        compiler_params=pltpu.CompilerParams(dimension_semantics=("arbitrary",),
                                             vmem_limit_bytes=VMEM_LIMIT),
        name="inproj_prompt",
    )(*args)


def _ln_swish(y, lg, lb):
    mu = jnp.mean(y, axis=-1, keepdims=True)
    d = y - mu
    var = jnp.mean(d * d, axis=-1, keepdims=True)
    yn = (d * lax.rsqrt(var + EPS)) * lg + lb
    return yn * jax.nn.sigmoid(yn)


def _conv_tile(t, u_ref, w_ref, b_ref, lg_ref, lb_ref, y_ref, buf_ref, tc):
    @pl.when(t == 0)
    def _():
        buf_ref[0:CONV_PAD, :] = jnp.zeros((CONV_PAD, CONV_CH), F32)

    @pl.when(t > 0)
    def _():
        buf_ref[0:CONV_PAD, :] = buf_ref[tc:tc + CONV_PAD, :]

    buf_ref[CONV_PAD:CONV_PAD + tc, :] = u_ref[...]
    first_tap = CONV_PAD - CONV_HIST

    shifts = [first_tap + j for j in range(CONV_WIDTH)]
    by_rho = {rho: [s for s in shifts if s % SUBLANES == rho] for rho in range(SUBLANES)}
    win = max(shifts) + CONV_ROWS
    for base in range(0, tc, CONV_ROWS):
        accs = []
        for lt in range(CONV_CH // LANES):
            cols = slice(lt * LANES, (lt + 1) * LANES)
            window = buf_ref[base:base + win, cols]
            acc = jnp.broadcast_to(b_ref[:, cols], (CONV_ROWS, LANES))
            for rho, group in by_rho.items():
                if not group:
                    continue
                shifted = pltpu.roll(window, win - rho, 0) if rho else window
                for s in group:
                    j = s - first_tap
                    acc = acc + w_ref[j:j + 1, cols] * shifted[s - rho:s - rho + CONV_ROWS, :]
            accs.append(acc)
        y = jnp.concatenate(accs, axis=1)
        y_ref[base:base + CONV_ROWS, :] = _ln_swish(y, lg_ref[...], lb_ref[...]).astype(BF16)


def _conv_sample_body(u_ref, st_ref, w_ref, b_ref, lg_ref, lb_ref, y_ref, hist_ref, buf_ref, *, ts):
    buf_ref[:, 0:CONV_HIST, :] = st_ref[...]
    buf_ref[:, CONV_HIST:CONV_HIST + ts, :] = u_ref[...]
    nseq = u_ref.shape[0]
    acc = jnp.broadcast_to(b_ref[...][None], (nseq, ts, CONV_CH))
    for j in range(CONV_WIDTH):
        acc = acc + w_ref[j:j + 1, :][None] * buf_ref[:, j:j + ts, :]
    y_ref[...] = _ln_swish(acc, lg_ref[...][None], lb_ref[...][None])
    hist_ref[...] = buf_ref[:, ts:ts + CONV_HIST, :]


def _conv_sample(u, state, w, b, lg, lb):
    nseq, ts, _ = u.shape
    full = lambda s: pl.BlockSpec(s, lambda i: (0,) * len(s))
    return pl.pallas_call(
        functools.partial(_conv_sample_body, ts=ts),
        grid=(1,),
        in_specs=[full(u.shape), full(state.shape), full(w.shape), full(b.shape), full(lg.shape),
                  full(lb.shape)],
        out_specs=[full(u.shape), full(state.shape)],
        out_shape=[jax.ShapeDtypeStruct(u.shape, F32), jax.ShapeDtypeStruct(state.shape, F32)],
        scratch_shapes=[pltpu.VMEM((nseq, CONV_HIST + ts, CONV_CH), F32)],
        compiler_params=pltpu.CompilerParams(dimension_semantics=("arbitrary",),
                                             vmem_limit_bytes=VMEM_LIMIT),
        name="conv_sample",
    )(u, state, w, b, lg, lb)


def _attn_prompt_body(kbias_ref, q_ref, k_ref, v_ref, tri_ref, o_ref, qm_ref, run_ref, acc_ref, d_ref,
                      tot_ref):
    qs = pl.program_id(2)
    blk = ATT_BLOCK
    tq = blk * ATT_QSUB
    tri = tri_ref[...]
    q2 = q_ref[...]
    lane = lax.broadcasted_iota(jnp.int32, (tq, LANES), 1)
    for h in range(HEADS_PER_LANE_TILE):
        in_head = (lane >= h * HEAD_DIM) & (lane < (h + 1) * HEAD_DIM)
        qm_ref[h * tq:(h + 1) * tq, 0:LANES] = jnp.where(in_head, q2, jnp.zeros_like(q2))
        pick = (lane >= BIAS_PARTS * h) & (lane < BIAS_PARTS * (h + 1))
        qm_ref[h * tq:(h + 1) * tq, LANES:2 * LANES] = jnp.where(pick, 1.0, 0.0).astype(BF16)
    run_ref[...] = jnp.zeros_like(run_ref)
    acc_ref[...] = jnp.zeros_like(acc_ref)
    kbias = kbias_ref[...]

    def rows_of(sub_lo):
        return [slice(h * tq + sub_lo * blk, (h + 1) * tq) for h in range(HEADS_PER_LANE_TILE)]

    def scores(j, sub_lo, causal):
        n = tq - sub_lo * blk
        start = pl.multiple_of(j * blk, blk)
        kj = jnp.concatenate([k_ref[pl.ds(start, blk), :], kbias], axis=1)
        qm = jnp.concatenate([qm_ref[r, :] for r in rows_of(sub_lo)], axis=0)
        z = lax.dot_general(qm, kj, (((1,), (1,)), ((), ())), preferred_element_type=F32)
        c = _softplus2(z)
        if causal is not None:
            c = jnp.where(causal, c, 0.0)
        later = jnp.dot(c.astype(BF16), tri, preferred_element_type=F32)
        total = jnp.broadcast_to(jnp.sum(c, axis=-1, keepdims=True), (HEADS_PER_LANE_TILE * n, LANES))
        return (z - c) - later, total

    def apply(j, sub_lo, d, total, causal):
        n = tq - sub_lo * blk
        rows = rows_of(sub_lo)
        start = pl.multiple_of(j * blk, blk)
        vj = v_ref[pl.ds(start, blk), :]
        run = jnp.concatenate([run_ref[r, :] for r in rows], axis=0)
        a = _exp2(d - jnp.concatenate([run, run], axis=1))
        if causal is not None:
            a = jnp.where(causal, a, 0.0)
        pv = jnp.dot(a.astype(BF16), vj, preferred_element_type=F32)
        run = run + total
        for h, r in enumerate(rows):
            acc_ref[r, :] += pv[h * n:(h + 1) * n]
            run_ref[r, :] = run[h * n:(h + 1) * n]

    base = qs * ATT_QSUB
    for jj in reversed(range(ATT_QSUB)):
        n = tq - jj * blk
        r = lax.broadcasted_iota(jnp.int32, (n, blk), 0)
        col = lax.broadcasted_iota(jnp.int32, (n, blk), 1)
        causal = col < r
        causal = jnp.concatenate([causal] * HEADS_PER_LANE_TILE, axis=0)
        d, total = scores(base + jj, jj, causal)
        apply(base + jj, jj, d, total, causal)

    def stage(j):
        d, total = scores(j, 0, None)
        d_ref[...] = d
        tot_ref[...] = total

    def drain(j):
        apply(j, 0, d_ref[...], tot_ref[...], None)

    @pl.when(qs > 0)
    def _():
        stage(base - 1)

        def body(i, carry):
            drain(base - 1 - i)
            stage(base - 2 - i)
            return carry

        lax.fori_loop(0, base - 1, body, 0)
        drain(0)

    lane_head0 = lane < HEAD_DIM
    o_ref[...] = jnp.where(lane_head0, acc_ref[0:tq, :], acc_ref[tq:2 * tq, :]).astype(BF16)


def _attn_prompt(bias, q, kb, vb, tri):
    bsz, t, _ = q.shape
    blk = ATT_BLOCK
    tq = blk * ATT_QSUB
    nrow = tq * HEADS_PER_LANE_TILE
    hi = bias.astype(BF16)
    lo = (bias - hi.astype(F32)).astype(BF16)
    parts = jnp.stack([hi, lo], axis=1).reshape(N_HEAD_PAIRS, HEADS_PER_LANE_TILE * BIAS_PARTS)
    kbias = jnp.pad(parts, ((0, 0), (0, LANES - parts.shape[1])))
    kbias = jnp.broadcast_to(kbias[:, None, :], (N_HEAD_PAIRS, blk, LANES))
    return pl.pallas_call(
        _attn_prompt_body,
        grid=(bsz, N_HEAD_PAIRS, t // tq),
        in_specs=[pl.BlockSpec((None, blk, LANES), lambda b, h, i: (h, 0, 0)),
                  pl.BlockSpec((None, tq, LANES), lambda b, h, i: (b, i, h)),
                  pl.BlockSpec((None, t, LANES), lambda b, h, i: (b, 0, h)),
                  pl.BlockSpec((None, t, LANES), lambda b, h, i: (b, 0, h)),
                  _const_spec((blk, blk))],
        out_specs=pl.BlockSpec((None, tq, LANES), lambda b, h, i: (b, i, h)),
        out_shape=jax.ShapeDtypeStruct((bsz, t, ATT_WIDTH), BF16),
        scratch_shapes=[pltpu.VMEM((nrow, 2 * LANES), BF16), pltpu.VMEM((nrow, LANES), F32),
                        pltpu.VMEM((nrow, LANES), F32), pltpu.VMEM((nrow, blk), F32),
                        pltpu.VMEM((nrow, LANES), F32)],
        compiler_params=pltpu.CompilerParams(
            dimension_semantics=("arbitrary", "arbitrary", "arbitrary"), vmem_limit_bytes=VMEM_LIMIT),
        name="attn_prompt",
    )(kbias, q, kb, vb, tri)


def _attn_sample_body(pt_ref, qbd_ref, biasr_ref, knew_ref, vnew_ref, tri_ref, hm_ref, *rest, ts):
    del pt_ref
    k_refs = rest[:PAGES_PER_STEP]
    v_refs = rest[PAGES_PER_STEP:2 * PAGES_PER_STEP]
    o_ref, run_ref, acc_ref = rest[2 * PAGES_PER_STEP:]
    g = pl.program_id(1)
    qbd = qbd_ref[...]
    biasr = biasr_ref[...]
    tri = tri_ref[...]

    nrow = ts * N_HEADS
    nt = (((1,), (1,)), ((), ()))

    @pl.when(g == 0)
    def _():
        key = lax.broadcasted_iota(jnp.int32, (nrow, PAGE_SIZE), 1)
        qidx = lax.broadcasted_iota(jnp.int32, (nrow, PAGE_SIZE), 0) // N_HEADS
        mask = (key < qidx) & (key < ts)
        z = jnp.dot(qbd, knew_ref[...], preferred_element_type=F32) + biasr
        c = jnp.where(mask, _softplus2(z), 0.0)
        sums = jnp.dot(c.astype(BF16), tri, preferred_element_type=F32)
        a = jnp.where(mask, _exp2((z - c) - sums[:, :PAGE_SIZE]), 0.0)
        acc_ref[...] = lax.dot_general(a, vnew_ref[...], nt, preferred_element_type=F32)
        run_ref[...] = sums[:, PAGE_SIZE:]

    order = list(reversed(range(PAGES_PER_STEP)))
    zs = [jnp.dot(qbd, k_refs[p][...], preferred_element_type=F32) + biasr for p in order]
    cs = [_softplus2(z) for z in zs]
    sums = jnp.dot(jnp.concatenate(cs, axis=0).astype(BF16), tri, preferred_element_type=F32)
    run = run_ref[...]
    acc = acc_ref[...]
    for n, p in enumerate(order):
        rows = slice(n * nrow, (n + 1) * nrow)
        a = _exp2((zs[n] - cs[n]) - sums[rows, :PAGE_SIZE] - run)
        acc = acc + lax.dot_general(a, v_refs[p][...], nt, preferred_element_type=F32)
        run = run + sums[rows, PAGE_SIZE:]
    run_ref[...] = run
    acc_ref[...] = acc

    @pl.when(g == pl.num_programs(1) - 1)
    def _():
        res = acc.reshape(ts, N_HEADS, ATT_WIDTH) * hm_ref[...][None]
        o_ref[...] = jnp.sum(res, axis=1)


def _attn_sample(page_table, qbd, biasr, knew, vnew, tri, hm, cache_kt, cache_vt, layer, ts):
    nseq, n_pages = page_table.shape
    n_steps = n_pages // PAGES_PER_STEP
    nrow = ts * N_HEADS

    def page_spec(p):
        def imap(b, g, pt):
            return (layer, pt[b, (n_steps - 1 - g) * PAGES_PER_STEP + p], 0, 0)
        return pl.BlockSpec((None, None, ATT_WIDTH, PAGE_SIZE), imap)

    per_seq = lambda s: pl.BlockSpec((None,) + s, lambda b, g, pt: (b,) + (0,) * len(s))
    const = lambda s: pl.BlockSpec(s, lambda b, g, pt: (0,) * len(s))
    grid_spec = pltpu.PrefetchScalarGridSpec(
        num_scalar_prefetch=1,
        grid=(nseq, n_steps),
        in_specs=[per_seq((nrow, ATT_WIDTH)), const((nrow, PAGE_SIZE)),
                  per_seq((ATT_WIDTH, PAGE_SIZE)), per_seq((ATT_WIDTH, PAGE_SIZE)),
                  const((PAGE_SIZE, 2 * PAGE_SIZE)), const((N_HEADS, ATT_WIDTH))]
                 + [page_spec(p) for p in range(PAGES_PER_STEP)] * 2,
        out_specs=per_seq((ts, ATT_WIDTH)),
        scratch_shapes=[pltpu.VMEM((nrow, PAGE_SIZE), F32), pltpu.VMEM((nrow, ATT_WIDTH), F32)],
    )
    return pl.pallas_call(
        functools.partial(_attn_sample_body, ts=ts),
        grid_spec=grid_spec,
        out_shape=jax.ShapeDtypeStruct((nseq, ts, ATT_WIDTH), F32),
        compiler_params=pltpu.CompilerParams(dimension_semantics=("arbitrary", "arbitrary"),
                                             vmem_limit_bytes=VMEM_LIMIT),
        name="attn_sample",
    )(page_table, qbd, biasr, knew, vnew, tri, hm,
      *([cache_kt] * PAGES_PER_STEP), *([cache_vt] * PAGES_PER_STEP))


def _mix_body(x_ref, y_ref, att_ref, gpre_ref, gpost_ref, wg_ref, wc_ref, wa_ref, wo_ref, o_ref):
    x = x_ref[...]
    h = _rms(x, gpre_ref[...]).astype(BF16)
    gates = jnp.dot(h, wg_ref[...], preferred_element_type=F32)
    conv_d = jnp.dot(y_ref[...].astype(BF16), wc_ref[...], preferred_element_type=F32)
    att_d = jnp.dot(att_ref[...].astype(BF16), wa_ref[...], preferred_element_type=F32)
    merged = (jax.nn.sigmoid(gates[:, :D_MODEL]) * conv_d
              + jax.nn.sigmoid(gates[:, D_MODEL:]) * att_d)
    mo = jnp.dot(merged.astype(BF16), wo_ref[...], preferred_element_type=F32)
    o_ref[...] = x + _rms(mo, gpost_ref[...])


def _mix(x, y, att, gpre, gpost, wg, wc, wa, wo, tm):
    t = x.shape[0]
    tspec = lambda w: pl.BlockSpec((tm, w), lambda i: (i, 0))
    return pl.pallas_call(
        _mix_body,
        grid=(t // tm,),
        in_specs=[tspec(D_MODEL), tspec(CONV_CH), tspec(ATT_WIDTH),
                  _const_spec((1, D_MODEL)), _const_spec((1, D_MODEL)),
                  _const_spec(wg.shape), _const_spec(wc.shape), _const_spec(wa.shape),
                  _const_spec(wo.shape)],
        out_specs=tspec(D_MODEL),
        out_shape=jax.ShapeDtypeStruct((t, D_MODEL), F32),
        compiler_params=pltpu.CompilerParams(dimension_semantics=("arbitrary",),
                                             vmem_limit_bytes=VMEM_LIMIT),
        name="mix",
    )(x, y, att, gpre, gpost, wg, wc, wa, wo)


def _mix_conv_body(x_ref, u_ref, att_ref, gpre_ref, gpost_ref, wg_ref, wc_ref, wa_ref, wo_ref,
                   cw_ref, cb_ref, lg_ref, lb_ref, o_ref, buf_ref, y_ref, *, tm, per_seq):
    _conv_tile(pl.program_id(0) % per_seq, u_ref, cw_ref, cb_ref, lg_ref, lb_ref, y_ref, buf_ref, tm)
    _mix_body(x_ref, y_ref, att_ref, gpre_ref, gpost_ref, wg_ref, wc_ref, wa_ref, wo_ref, o_ref)


def _mix_conv(x, u, att, gpre, gpost, wg, wc, wa, wo, cw, cb, lg, lb, tm, bsz):
    t = x.shape[0]
    per_seq = t // bsz // tm
    tspec = lambda w: pl.BlockSpec((tm, w), lambda i: (i, 0))
    return pl.pallas_call(
        functools.partial(_mix_conv_body, tm=tm, per_seq=per_seq),
        grid=(t // tm,),
        in_specs=[tspec(D_MODEL), tspec(CONV_CH), tspec(ATT_WIDTH),
                  _const_spec((1, D_MODEL)), _const_spec((1, D_MODEL)),
                  _const_spec(wg.shape), _const_spec(wc.shape), _const_spec(wa.shape),
                  _const_spec(wo.shape), _const_spec((CONV_WIDTH, CONV_CH)), _const_spec((1, CONV_CH)),
                  _const_spec((1, CONV_CH)), _const_spec((1, CONV_CH))],
        out_specs=tspec(D_MODEL),
        out_shape=jax.ShapeDtypeStruct((t, D_MODEL), F32),
        scratch_shapes=[pltpu.VMEM((CONV_PAD + tm, CONV_CH), F32), pltpu.VMEM((tm, CONV_CH), BF16)],
        compiler_params=pltpu.CompilerParams(dimension_semantics=("arbitrary",),
                                             vmem_limit_bytes=VMEM_LIMIT),
        name="mix_conv",
    )(x, u, att, gpre, gpost, wg, wc, wa, wo, cw, cb, lg, lb)


def _ffn_body(x_ref, gpre_ref, gpost_ref, w1_ref, w2_ref, o_ref):
    x = x_ref[...]
    h = _rms(x, gpre_ref[...]).astype(BF16)
    hf = jnp.dot(h, w1_ref[...].astype(BF16), preferred_element_type=F32)
    a = hf[:, :D_FF]
    ff = (a * jax.nn.sigmoid(a)) * hf[:, D_FF:]
    out = jnp.dot(ff.astype(BF16), w2_ref[...].astype(BF16), preferred_element_type=F32)
    o_ref[...] = x + _rms(out, gpost_ref[...])


def _layer_spec(w, layer):
    return pl.BlockSpec((None,) + w.shape[1:], lambda *_: (layer,) + (0,) * (w.ndim - 1),
                        pipeline_mode=pl.Buffered(1))


def _ffn(x, gpre, gpost, w1, w2, layer, tm):
    t = x.shape[0]
    tspec = pl.BlockSpec((tm, D_MODEL), lambda i: (i, 0))
    return pl.pallas_call(
        _ffn_body,
        grid=(t // tm,),
        in_specs=[tspec, _const_spec((1, D_MODEL)), _const_spec((1, D_MODEL)),
                  _layer_spec(w1, layer), _layer_spec(w2, layer)],
        out_specs=tspec,
        out_shape=jax.ShapeDtypeStruct((t, D_MODEL), F32),
        compiler_params=pltpu.CompilerParams(dimension_semantics=("arbitrary",),
                                             vmem_limit_bytes=VMEM_LIMIT),
        name="ffn",
    )(x, gpre, gpost, w1, w2)


def _strict_upper(n):
    r = lax.broadcasted_iota(jnp.int32, (n, n), 0)
    c = lax.broadcasted_iota(jnp.int32, (n, n), 1)
    return (r > c).astype(BF16)


def kernel(x_prompt, x_sample, cache_k, cache_v, state_conv, page_table, norm_pre_mix, norm_post_mix,
           norm_pre_ffn, norm_post_ffn, w_in, conv_dw, conv_b, conv_ln_g, conv_ln_b, w_conv_out,
           att_bias, w_att_out, w_o, w_ffn_in, w_ffn_out):
    depth = w_in.shape[0]
    bp, tp, _ = x_prompt.shape
    bs, ts, _ = x_sample.shape
    n_phys = cache_k.shape[1]
    tm_p = 512
    tm_s = bs * ts

    hp = x_prompt.reshape(bp * tp, D_MODEL)
    hs = x_sample.reshape(bs * ts, D_MODEL)
    ckt = jnp.transpose(cache_k, (0, 1, 3, 4, 2)).reshape(depth, n_phys, ATT_WIDTH, PAGE_SIZE)
    cvt = jnp.transpose(cache_v, (0, 1, 3, 4, 2)).reshape(depth, n_phys, ATT_WIDTH, PAGE_SIZE)

    tri_p = _strict_upper(ATT_BLOCK)
    tri_s = jnp.concatenate([_strict_upper(PAGE_SIZE), jnp.ones((PAGE_SIZE, PAGE_SIZE), BF16)], axis=1)
    head_of_col = jnp.arange(ATT_WIDTH, dtype=jnp.int32) // HEAD_DIM
    hm = (head_of_col[None, :] == jnp.arange(N_HEADS, dtype=jnp.int32)[:, None]).astype(F32)

    row = lambda a: a.reshape(1, -1)
    cp_l, ks_l, vs_l, cs_l = [], [], [], []
    kt_all = vt_all = None
    gates_at = 2 * CONV_CH + 3 * ATT_WIDTH
    for l in range(depth):
        wg = w_in[l, :, gates_at:].astype(BF16)
        wc = w_conv_out[l].astype(BF16)
        wa = w_att_out[l].astype(BF16)
        wo = w_o[l].astype(BF16)
        gpre, gpost = row(norm_pre_mix[l]), row(norm_post_mix[l])
        gpre_f, gpost_f = row(norm_pre_ffn[l]), row(norm_post_ffn[l])
        cw, cb = conv_dw[l], row(conv_b[l])
        lg, lb = row(conv_ln_g[l]), row(conv_ln_b[l])
        bias = att_bias[l].astype(F32) * LOG2E

        u, q, kt_all, vt_all, kb, vb = _inproj_prompt(hp, gpre, w_in, tm_p, l, depth, bp, kt_all, vt_all)
        att = _attn_prompt(bias, q.reshape(bp, tp, ATT_WIDTH), kb.reshape(bp, tp, ATT_WIDTH),
                           vb.reshape(bp, tp, ATT_WIDTH), tri_p)
        hp = _mix_conv(hp, u, att.reshape(bp * tp, ATT_WIDTH), gpre, gpost, wg, wc, wa, wo,
                       cw, cb, lg, lb, tm_p, bp)
        hist = u.reshape(bp, tp, CONV_CH)[:, tp - CONV_HIST:, :]
        hp = _ffn(hp, gpre_f, gpost_f, w_ffn_in, w_ffn_out, l, 256)
        cp_l.append(hist)

        u, q, k, v = _inproj(hs, gpre, w_in, l, tm_s)
        y, hist = _conv_sample(u.reshape(bs, ts, CONV_CH), state_conv[l], cw, cb, lg, lb)
        qbd = (q.astype(F32).reshape(bs, ts, 1, ATT_WIDTH) * hm[None, None]).reshape(
            bs, ts * N_HEADS, ATT_WIDTH)
        biasr = jnp.broadcast_to(jnp.tile(bias, ts)[:, None], (ts * N_HEADS, PAGE_SIZE))
        pad_keys = ((0, 0), (0, 0), (0, PAGE_SIZE - ts))
        knew = jnp.pad(jnp.swapaxes(k.reshape(bs, ts, ATT_WIDTH), 1, 2), pad_keys)
        vnew = jnp.pad(jnp.swapaxes(v.reshape(bs, ts, ATT_WIDTH), 1, 2), pad_keys)
        att = _attn_sample(page_table, qbd, biasr, knew, vnew, tri_s, hm, ckt, cvt, l, ts)
        hs = _mix(hs, y.reshape(bs * ts, CONV_CH), att.reshape(bs * ts, ATT_WIDTH),
                  gpre, gpost, wg, wc, wa, wo, tm_s)
        hs = _ffn(hs, gpre_f, gpost_f, w_ffn_in, w_ffn_out, l, tm_s)
        ks_l.append(k.reshape(bs, ts, N_HEADS, HEAD_DIM))
        vs_l.append(v.reshape(bs, ts, N_HEADS, HEAD_DIM))
        cs_l.append(hist)

    unstack = lambda a: jnp.transpose(a.reshape(depth, bp, N_HEADS, HEAD_DIM, tp), (0, 1, 4, 2, 3))
    return (hp.reshape(bp, tp, D_MODEL), hs.reshape(bs, ts, D_MODEL),
            unstack(kt_all), unstack(vt_all), jnp.stack(cp_l),
            jnp.stack(ks_l), jnp.stack(vs_l), jnp.stack(cs_l))
```
